```python
import math
import jax, jax.numpy as jnp
from jax import lax
import numpy as np

D_MODEL = 1024
BATCH = 1
SEQ = 16384
DEPTH = 1
DEC_BATCH = 128
DEC_SEQ = 1
PAST_LEN = 8192
PAGE_SIZE = 128

N_META = 16
D_MIX = D_MODEL
D_ATT = D_MIX // 2
D_LRU = D_MIX - D_ATT
N_ATT_HEADS = 4
HEAD_DIM = D_ATT // (2 * N_ATT_HEADS)
N_LRU_BLOCKS = 8
LRU_BLOCK = D_LRU // N_LRU_BLOCKS
CONV_WIDTH = 4
LRU_C = 8.0
ROPE_THETA = 10000.0
Q_BLOCK = 128
EPS = 1e-6
D_IN = 4 * D_ATT + 2 * D_LRU

kernel_name = "hymba_style_rglru_diffattn_decode_step"


def rmsnorm(x, g):
    xf = x.astype(jnp.float32)
    y = xf * lax.rsqrt(jnp.mean(xf * xf, axis=-1, keepdims=True) + EPS)
    return (y * g.astype(jnp.float32)).astype(x.dtype)


def rope(x, pos):
    half = HEAD_DIM // 2
    inv = ROPE_THETA ** (-jnp.arange(0, HEAD_DIM, 2, dtype=jnp.float32) / HEAD_DIM)
    ang = pos.astype(jnp.float32)[:, None] * inv[None, :]
    cos = jnp.cos(ang)[:, None, None, :]
    sin = jnp.sin(ang)[:, None, None, :]
    xf = x.astype(jnp.float32)
    x1, x2 = xf[..., :half], xf[..., half:]
    out = jnp.concatenate([x1 * cos - x2 * sin, x2 * cos + x1 * sin], axis=-1)
    return out.astype(x.dtype)


def diff_lambda(lq1, lk1, lq2, lk2, lam_init):
    f = jnp.float32
    return (jnp.exp(jnp.sum(lq1.astype(f) * lk1.astype(f)))
            - jnp.exp(jnp.sum(lq2.astype(f) * lk2.astype(f))) + lam_init)


def prologue(x, pos, norm_pre, w_in):
    B, T, _ = x.shape
    u = rmsnorm(x, norm_pre)
    z = u @ w_in
    q, k, v, g_att, xb, g_lru = jnp.split(
        z, [D_ATT, 2 * D_ATT, 3 * D_ATT, 4 * D_ATT, 4 * D_ATT + D_LRU], axis=-1)
    q = rope(q.reshape(B, T, N_ATT_HEADS, 2, HEAD_DIM), pos)
    k = rope(k.reshape(B, T, N_ATT_HEADS, 2, HEAD_DIM), pos)
    v = v.reshape(B, T, N_ATT_HEADS, 2 * HEAD_DIM)
    return q, k, v, g_att, xb, g_lru


def diff_attn(q, k, v, q_pos, k_pos, lam):
    f = jnp.float32
    s = jnp.einsum('bqhcd,bkhcd->bhcqk', q.astype(f), k.astype(f)) * (HEAD_DIM ** -0.5)
    mask = k_pos[None, :] <= q_pos[:, None]
    s = jnp.where(mask, s, -jnp.inf)
    p = jax.nn.softmax(s, axis=-1)
    p = p[:, :, 0] - lam * p[:, :, 1]
    return jnp.einsum('bhqk,bkhe->bqhe', p, v.astype(f))


def prompt_attention(q, k, v, lam):
    B, T = q.shape[0], q.shape[1]
    n_blk = -(-T // Q_BLOCK)
    t_pad = n_blk * Q_BLOCK
    pad = t_pad - T
    qp = jnp.pad(q, ((0, 0), (0, pad), (0, 0), (0, 0), (0, 0)))
    kp = jnp.pad(k, ((0, 0), (0, pad), (0, 0), (0, 0), (0, 0)))
    vp = jnp.pad(v, ((0, 0), (0, pad), (0, 0), (0, 0)))
    k_pos = jnp.arange(t_pad, dtype=jnp.int32)

    def block(i):
        start = i * Q_BLOCK
        qb = lax.dynamic_slice_in_dim(qp, start, Q_BLOCK, axis=1)
        q_pos = start + jnp.arange(Q_BLOCK, dtype=jnp.int32)
        return diff_attn(qb, kp, vp, q_pos, k_pos, lam)

    o = lax.map(block, jnp.arange(n_blk, dtype=jnp.int32))
    o = jnp.moveaxis(o, 0, 1).reshape(B, t_pad, N_ATT_HEADS, 2 * HEAD_DIM)
    return o[:, :T]


def causal_conv(xb, buf, conv_w, conv_b):
    T = xb.shape[1]
    xp = jnp.concatenate([buf.astype(xb.dtype), xb], axis=1)
    y = conv_b
    for j in range(CONV_WIDTH):
        y = y + conv_w[j] * xp[:, j:j + T]
    return y, xp[:, T:]


def rg_lru(xc, h0, w_r, b_r, w_i, b_i, lru_lambda):
    B, T, _ = xc.shape
    f = jnp.float32
    xf = xc.astype(f)
    xblk = xf.reshape(B, T, N_LRU_BLOCKS, LRU_BLOCK)
    r = jax.nn.sigmoid(jnp.einsum('btgi,gij->btgj', xblk, w_r.astype(f)).reshape(B, T, D_LRU)
                       + b_r.astype(f))
    i = jax.nn.sigmoid(jnp.einsum('btgi,gij->btgj', xblk, w_i.astype(f)).reshape(B, T, D_LRU)
                       + b_i.astype(f))
    log_a = -LRU_C * r * jax.nn.softplus(-lru_lambda.astype(f))
    a = jnp.exp(log_a)
    u = jnp.sqrt(-jnp.expm1(2.0 * log_a)) * (i * xf)

    def step(h, au):
        a_t, u_t = au
        h = a_t * h + u_t
        return h, h

    h_last, hs = lax.scan(step, h0.astype(f), (jnp.swapaxes(a, 0, 1), jnp.swapaxes(u, 0, 1)))
    return jnp.swapaxes(hs, 0, 1).astype(xc.dtype), h_last


def epilogue(x, o_att, y_lru, g_att, g_lru, lam_init, attn_subnorm, w_out, norm_post):
    B, T, _ = x.shape
    o = rmsnorm(o_att, attn_subnorm) * (1.0 - lam_init)
    o = o.reshape(B, T, D_ATT).astype(x.dtype) * jax.nn.silu(g_att)
    y = y_lru * jax.nn.silu(g_lru)
    mix = jnp.concatenate([o, y], axis=-1) @ w_out
    return x + rmsnorm(mix, norm_post)


def setup_inputs(seed: int = 0) -> dict:
    key = jax.random.key(seed)
    ks = jax.random.split(key, 24)
    f = jnp.float32
    n_pages = PAST_LEN // PAGE_SIZE
    n_pool = (DEC_BATCH * n_pages * 5) // 4
    nrm = lambda k, shape, s: jax.random.normal(k, shape, f) * s
    a_c = jax.random.uniform(ks[14], (DEPTH, D_LRU), f, 0.9, 0.999)
    a0 = a_c ** (1.0 / LRU_C)
    lru_lambda = jnp.log(a0) - jnp.log1p(-a0)
    page_table = jax.random.permutation(ks[6], n_pool)[:DEC_BATCH * n_pages]
    page_table = page_table.reshape(DEC_BATCH, n_pages).astype(jnp.int32)
    return {
        "x_prompt": nrm(ks[0], (BATCH, SEQ, D_MODEL), 1.0),
        "x_sample": nrm(ks[1], (DEC_BATCH, DEC_SEQ, D_MODEL), 1.0),
        "cache_k": nrm(ks[2], (DEPTH, n_pool, PAGE_SIZE, N_ATT_HEADS, 2 * HEAD_DIM), 1.0),
        "cache_v": nrm(ks[3], (DEPTH, n_pool, PAGE_SIZE, N_ATT_HEADS, 2 * HEAD_DIM), 1.0),
        "state_h": nrm(ks[4], (DEPTH, DEC_BATCH, D_LRU), 0.5),
        "state_conv": nrm(ks[5], (DEPTH, DEC_BATCH, CONV_WIDTH - 1, D_LRU), 1.0),
        "page_table": page_table,
        "meta_tokens": nrm(ks[7], (N_META, D_MODEL), 1.0),
        "norm_pre": 1.0 + nrm(ks[8], (DEPTH, D_MODEL), 0.01),
        "w_in": nrm(ks[9], (DEPTH, D_MODEL, D_IN), D_MODEL ** -0.5),
        "conv_w": nrm(ks[10], (DEPTH, CONV_WIDTH, D_LRU), CONV_WIDTH ** -0.5),
        "conv_b": nrm(ks[11], (DEPTH, D_LRU), 0.01),
        "w_gate_r": nrm(ks[12], (DEPTH, N_LRU_BLOCKS, LRU_BLOCK, LRU_BLOCK), LRU_BLOCK ** -0.5),
        "b_gate_r": nrm(ks[13], (DEPTH, D_LRU), 0.01),
        "w_gate_i": nrm(ks[15], (DEPTH, N_LRU_BLOCKS, LRU_BLOCK, LRU_BLOCK), LRU_BLOCK ** -0.5),
        "b_gate_i": nrm(ks[16], (DEPTH, D_LRU), 0.01),
        "lru_lambda": lru_lambda,
        "lambda_q1": nrm(ks[17], (DEPTH, HEAD_DIM), 0.1),
        "lambda_k1": nrm(ks[18], (DEPTH, HEAD_DIM), 0.1),
        "lambda_q2": nrm(ks[19], (DEPTH, HEAD_DIM), 0.1),
        "lambda_k2": nrm(ks[20], (DEPTH, HEAD_DIM), 0.1),
        "attn_subnorm": 1.0 + nrm(ks[21], (DEPTH, 2 * HEAD_DIM), 0.01),
        "w_out": nrm(ks[22], (DEPTH, D_MIX, D_MODEL), D_MIX ** -0.5),
        "norm_post": 1.0 + nrm(ks[23], (DEPTH, D_MODEL), 0.01),
    }


def reference(x_prompt, x_sample, cache_k, cache_v, state_h, state_conv, page_table,
              meta_tokens, norm_pre, w_in, conv_w, conv_b, w_gate_r, b_gate_r,
              w_gate_i, b_gate_i, lru_lambda, lambda_q1, lambda_k1, lambda_q2, lambda_k2,
              attn_subnorm, w_out, norm_post):
    B = x_prompt.shape[0]
    DB, S_new = x_sample.shape[0], x_sample.shape[1]
    past_len = page_table.shape[1] * cache_k.shape[2]

    meta = jnp.broadcast_to(meta_tokens.astype(x_prompt.dtype)[None], (B, N_META, D_MODEL))
    hp = jnp.concatenate([meta, x_prompt], axis=1)
    T = hp.shape[1]
    pos_p = jnp.arange(T, dtype=jnp.int32)
    hs = x_sample
    pos_s = past_len + jnp.arange(S_new, dtype=jnp.int32)
    k_pos_s = jnp.arange(past_len + S_new, dtype=jnp.int32)

    k_p_l, v_p_l, h_p_l, c_p_l = [], [], [], []
    k_s_l, v_s_l, h_s_l, c_s_l = [], [], [], []
    for l in range(DEPTH):
        lam_init = 0.8 - 0.6 * math.exp(-0.3 * l)
        lam = diff_lambda(lambda_q1[l], lambda_k1[l], lambda_q2[l], lambda_k2[l], lam_init)

        q, k, v, g_att, xb, g_lru = prologue(hp, pos_p, norm_pre[l], w_in[l])
        o_att = prompt_attention(q, k, v, lam)
        buf0 = jnp.zeros((B, CONV_WIDTH - 1, D_LRU), xb.dtype)
        xc, cbuf = causal_conv(xb, buf0, conv_w[l], conv_b[l])
        h0 = jnp.zeros((B, D_LRU), jnp.float32)
        y_lru, h_last = rg_lru(xc, h0, w_gate_r[l], b_gate_r[l], w_gate_i[l], b_gate_i[l],
                               lru_lambda[l])
        hp = epilogue(hp, o_att, y_lru, g_att, g_lru, lam_init, attn_subnorm[l], w_out[l],
                      norm_post[l])
        k_p_l.append(k.reshape(B, T, N_ATT_HEADS, 2 * HEAD_DIM))
        v_p_l.append(v)
        h_p_l.append(h_last.astype(state_h.dtype))
        c_p_l.append(cbuf.astype(state_conv.dtype))

        q, k, v, g_att, xb, g_lru = prologue(hs, pos_s, norm_pre[l], w_in[l])
        past_k = cache_k[l][page_table].reshape(DB, past_len, N_ATT_HEADS, 2, HEAD_DIM)
        past_v = cache_v[l][page_table].reshape(DB, past_len, N_ATT_HEADS, 2 * HEAD_DIM)
        kk = jnp.concatenate([past_k.astype(k.dtype), k], axis=1)
        vv = jnp.concatenate([past_v.astype(v.dtype), v], axis=1)
        o_att = diff_attn(q, kk, vv, pos_s, k_pos_s, lam)
        xc, cbuf = causal_conv(xb, state_conv[l], conv_w[l], conv_b[l])
        y_lru, h_last = rg_lru(xc, state_h[l], w_gate_r[l], b_gate_r[l], w_gate_i[l],
                               b_gate_i[l], lru_lambda[l])
        hs = epilogue(hs, o_att, y_lru, g_att, g_lru, lam_init, attn_subnorm[l], w_out[l],
                      norm_post[l])
        k_s_l.append(k.reshape(DB, S_new, N_ATT_HEADS, 2 * HEAD_DIM))
        v_s_l.append(v)
        h_s_l.append(h_last.astype(state_h.dtype))
        c_s_l.append(cbuf.astype(state_conv.dtype))

    y_prompt = hp[:, N_META:]
    y_sample = hs
    k_prompt = jnp.stack(k_p_l)
    v_prompt = jnp.stack(v_p_l)
    h_prompt = jnp.stack(h_p_l)
    conv_prompt = jnp.stack(c_p_l)
    k_sample = jnp.stack(k_s_l)
    v_sample = jnp.stack(v_s_l)
    h_sample = jnp.stack(h_s_l)
    conv_sample = jnp.stack(c_s_l)
    return (y_prompt, y_sample, k_prompt, v_prompt, h_prompt, conv_prompt,
            k_sample, v_sample, h_sample, conv_sample)
```

```python
import functools
import math

import jax
import jax.numpy as jnp
from jax import lax
from jax.experimental import pallas as pl
from jax.experimental.pallas import tpu as pltpu

N_META = 16
N_ATT_HEADS = 4
HEAD_DIM = 64
HEAD_V = 2 * HEAD_DIM
D_ATT = N_ATT_HEADS * HEAD_V
D_LRU = 512
N_LRU_BLOCKS = 8
CONV_WIDTH = 4
LRU_C = 8.0
ROPE_THETA = 10000.0
EPS = 1e-6
LAM_INIT = 0.8 - 0.6 * math.exp(-0.3 * 0)

LANES = 128
SUBLANES = 8
VMEM_LIMIT = 56 * 1024 * 1024

F32 = jnp.float32
BF16 = jnp.bfloat16


def _rms(x, g):
    return x * lax.rsqrt(jnp.mean(x * x, axis=-1, keepdims=True) + EPS) * g


def _silu(x):
    return x * jax.nn.sigmoid(x)


def _first_half_mask(shape):
    lane = lax.broadcasted_iota(jnp.int32, shape, len(shape) - 1)
    return (lane % HEAD_DIM) < (HEAD_DIM // 2)


def _rope_block(zb, cos, sin_signed, first_half):
    half = HEAD_DIM // 2
    partner = jnp.where(first_half, pltpu.roll(zb, LANES - half, axis=1), pltpu.roll(zb, half, axis=1))
    return zb * cos + partner * sin_signed


def _diff_lambda(lq1, lk1, lq2, lk2):
    return (jnp.exp(jnp.sum(lq1[...] * lk1[...])) - jnp.exp(jnp.sum(lq2[...] * lk2[...])) + LAM_INIT)


def _lru_gates(xc, wg_ref, br_ref, bi_ref, lam_ref):
    gates = jnp.dot(xc.astype(BF16), wg_ref[...], preferred_element_type=F32)
    r = jax.nn.sigmoid(gates[:, :D_LRU] + br_ref[...])
    i = jax.nn.sigmoid(gates[:, D_LRU:] + bi_ref[...])
    neg_lam = -lam_ref[...]
    softplus = jnp.maximum(neg_lam, 0.0) + jnp.log1p(jnp.exp(-jnp.abs(neg_lam)))
    log_a = -LRU_C * r * softplus
    a = jnp.exp(log_a)
    u = jnp.sqrt(1.0 - jnp.exp(2.0 * log_a)) * (i * xc)
    return a, u


def _prompt_pre_kernel(x_ref, meta_ref, npre_ref, win_ref, inv_ref, convw_ref, convb_ref, wg_ref,
                       br_ref, bi_ref, lam_ref,
                       q_ref, katt_ref, vt_ref, k_ref, v_ref, gatt_ref, ylg_ref,
                       kmatt_ref, vtm_ref, kmeta_ref, vmeta_ref, hlast_ref, convlast_ref,
                       cos_tab, sin_tab, conv_carry, h_carry, *, tr):
    step = pl.program_id(0)

    def rope_tables(pos0, rows):
        base = pos0.astype(F32) * inv_ref[...]
        cb, sb = jnp.cos(base), jnp.sin(base)
        cr, sr = cos_tab[0:rows, :], sin_tab[0:rows, :]
        cos = cr * cb - sr * sb
        sin = sr * cb + cr * sb
        return cos, jnp.where(_first_half_mask((rows, LANES)), -sin, sin)

    def project(x, pos0, rows):
        u = _rms(x, npre_ref[...]).astype(BF16)
        cos, sin_signed = rope_tables(pos0, rows)
        fh = _first_half_mask((rows, LANES))
        zq = jnp.dot(u, win_ref[:, 0:D_ATT], preferred_element_type=F32)
        zk = jnp.dot(u, win_ref[:, D_ATT:2 * D_ATT], preferred_element_type=F32)
        zv = jnp.dot(u, win_ref[:, 2 * D_ATT:3 * D_ATT], preferred_element_type=F32)
        qs, ks = [], []
        for h in range(N_ATT_HEADS):
            sl = slice(h * HEAD_V, (h + 1) * HEAD_V)
            qs.append(_rope_block(zq[:, sl], cos, sin_signed, fh) * (HEAD_DIM ** -0.5))
            ks.append(_rope_block(zk[:, sl], cos, sin_signed, fh))
        return u, qs, ks, zv

    def lru(xb, glru, rows, store_y):
        xp = jnp.concatenate([conv_carry[...], xb], axis=0)
        off = SUBLANES - (CONV_WIDTH - 1)
        xc = convb_ref[...] + convw_ref[0:1, :] * xp[off:off + rows]
        for j in range(1, CONV_WIDTH):
            xc = xc + convw_ref[j:j + 1, :] * xp[off + j:off + j + rows]
        conv_carry[...] = xp[rows:rows + SUBLANES]
        a, u = _lru_gates(xc, wg_ref, br_ref, bi_ref, lam_ref)
        row = lax.broadcasted_iota(jnp.int32, (rows, D_LRU), 0) % SUBLANES
        d = 1
        while d < SUBLANES:
            valid = row >= d
            u = jnp.where(valid, a * pltpu.roll(u, d, axis=0) + u, u)
            a = jnp.where(valid, a * pltpu.roll(a, d, axis=0), a)
            d *= 2
        h = h_carry[...]
        for g in range(rows // SUBLANES):
            sl = slice(g * SUBLANES, (g + 1) * SUBLANES)
            hg = a[sl] * h + u[sl]
            if store_y:
                ylg_ref[sl, :] = hg * _silu(glru[sl])
            h = jnp.broadcast_to(hg[SUBLANES - 1:SUBLANES, :], (SUBLANES, D_LRU))
        h_carry[...] = h

    @pl.when(step == 0)
    def _():
        r = lax.broadcasted_iota(jnp.int32, (tr, LANES), 0).astype(F32)
        ang = r * inv_ref[...]
        cos_tab[...] = jnp.cos(ang)
        sin_tab[...] = jnp.sin(ang)
        conv_carry[...] = jnp.zeros_like(conv_carry)
        h_carry[...] = jnp.zeros_like(h_carry)
        xm = jnp.concatenate([meta_ref[...], jnp.zeros((LANES - N_META, meta_ref.shape[1]), F32)], axis=0)
        u, _, ks, zv = project(xm, jnp.zeros((), jnp.int32), LANES)
        for h in range(N_ATT_HEADS):
            sl = slice(h * HEAD_V, (h + 1) * HEAD_V)
            kmeta_ref[:, sl] = ks[h]
            kmatt_ref[h] = ks[h].astype(BF16)
            vtm_ref[h] = zv[:, sl].T.astype(BF16)
        vmeta_ref[...] = zv
        um = u[0:N_META]
        xb = jnp.dot(um, win_ref[:, 4 * D_ATT:4 * D_ATT + D_LRU], preferred_element_type=F32)
        lru(xb, None, N_META, False)

    u, qs, ks, zv = project(x_ref[...], N_META + step * tr, tr)
    for h in range(N_ATT_HEADS):
        sl = slice(h * HEAD_V, (h + 1) * HEAD_V)
        q_ref[h] = qs[h].astype(BF16)
        k_ref[:, sl] = ks[h]
        katt_ref[h] = ks[h].astype(BF16)
        vt_ref[h] = zv[:, sl].T.astype(BF16)
    v_ref[...] = zv
    gatt_ref[...] = jnp.dot(u, win_ref[:, 3 * D_ATT:4 * D_ATT], preferred_element_type=F32)
    xb = jnp.dot(u, win_ref[:, 4 * D_ATT:4 * D_ATT + D_LRU], preferred_element_type=F32)
    glru = jnp.dot(u, win_ref[:, 4 * D_ATT + D_LRU:], preferred_element_type=F32)
    lru(xb, glru, tr, True)
    hlast_ref[...] = h_carry[...]
    convlast_ref[...] = conv_carry[...]


def _prompt_pre(x, meta, npre, win, inv, convw, convb, wg, br, bi, lam, *, tr):
    s, d = x.shape
    nt = s // tr
    full = lambda a: pl.BlockSpec(a.shape, lambda i: (0,) * a.ndim)
    out_shape = (
        jax.ShapeDtypeStruct((N_ATT_HEADS, s, HEAD_V), BF16),
        jax.ShapeDtypeStruct((N_ATT_HEADS, s, HEAD_V), BF16),
        jax.ShapeDtypeStruct((N_ATT_HEADS, HEAD_V, s), BF16),
        jax.ShapeDtypeStruct((s, D_ATT), F32),
        jax.ShapeDtypeStruct((s, D_ATT), F32),
        jax.ShapeDtypeStruct((s, D_ATT), F32),
        jax.ShapeDtypeStruct((s, D_LRU), F32),
        jax.ShapeDtypeStruct((N_ATT_HEADS, LANES, HEAD_V), BF16),
        jax.ShapeDtypeStruct((N_ATT_HEADS, HEAD_V, LANES), BF16),
        jax.ShapeDtypeStruct((LANES, D_ATT), F32),
        jax.ShapeDtypeStruct((LANES, D_ATT), F32),
        jax.ShapeDtypeStruct((SUBLANES, D_LRU), F32),
        jax.ShapeDtypeStruct((SUBLANES, D_LRU), F32),
    )
    out_specs = (
        pl.BlockSpec((N_ATT_HEADS, tr, HEAD_V), lambda i: (0, i, 0)),
        pl.BlockSpec((N_ATT_HEADS, tr, HEAD_V), lambda i: (0, i, 0)),
        pl.BlockSpec((N_ATT_HEADS, HEAD_V, tr), lambda i: (0, 0, i)),
        pl.BlockSpec((tr, D_ATT), lambda i: (i, 0)),
        pl.BlockSpec((tr, D_ATT), lambda i: (i, 0)),
        pl.BlockSpec((tr, D_ATT), lambda i: (i, 0)),
        pl.BlockSpec((tr, D_LRU), lambda i: (i, 0)),
    ) + tuple(pl.BlockSpec(o.shape, lambda i, n=len(o.shape): (0,) * n) for o in out_shape[7:])
    return pl.pallas_call(
        functools.partial(_prompt_pre_kernel, tr=tr),
        grid=(nt,),
        in_specs=[pl.BlockSpec((tr, d), lambda i: (i, 0))] + [full(a) for a in (meta, npre, win, inv, convw, convb, wg, br, bi, lam)],
        out_specs=out_specs,
        out_shape=out_shape,
        scratch_shapes=[pltpu.VMEM((tr, LANES), F32), pltpu.VMEM((tr, LANES), F32),
                        pltpu.VMEM((SUBLANES, D_LRU), F32), pltpu.VMEM((SUBLANES, D_LRU), F32)],
        compiler_params=pltpu.CompilerParams(dimension_semantics=("arbitrary",), vmem_limit_bytes=VMEM_LIMIT),
        name="prompt_pre",
    )(x, meta, npre, win, inv, convw, convb, wg, br, bi, lam)


def _prompt_attn_kernel(lq1, lk1, lq2, lk2, q_ref, k_ref, vt_ref, kmeta_ref, vtmeta_ref, o_ref,
                        m_sc, l_sc, acc_sc, *, tq):
    i = pl.program_id(1)
    qt = q_ref[0].astype(F32).T
    rows = lax.broadcasted_iota(jnp.int32, qt.shape, 0)
    qb = jnp.concatenate([jnp.where(rows < HEAD_DIM, qt, 0.0), jnp.where(rows >= HEAD_DIM, qt, 0.0)],
                         axis=1).astype(BF16)

    def update(kt, vtt, mask):
        s = jnp.dot(kt, qb, preferred_element_type=F32)
        if mask is not None:
            s = jnp.where(mask, s, -jnp.inf)
        m_old = m_sc[...]
        m_new = jnp.maximum(m_old, jnp.max(s, axis=0, keepdims=True))
        alpha = jnp.exp(m_old - m_new)
        p = jnp.exp(s - m_new)
        l_sc[...] = alpha * l_sc[...] + jnp.sum(p, axis=0, keepdims=True)
        acc_sc[...] = alpha * acc_sc[...] + jnp.dot(vtt, p.astype(BF16), preferred_element_type=F32)
        m_sc[...] = m_new

    m_sc[...] = jnp.full_like(m_sc, -jnp.inf)
    l_sc[...] = jnp.zeros_like(l_sc)
    acc_sc[...] = jnp.zeros_like(acc_sc)

    krow = lax.broadcasted_iota(jnp.int32, (LANES, 2 * tq), 0)
    update(kmeta_ref[0], vtmeta_ref[0], krow < N_META)

    def body(j, carry):
        start = pl.multiple_of(j * tq, tq)
        update(k_ref[0, pl.ds(start, tq), :], vt_ref[0, :, pl.ds(start, tq)], None)
        return carry

    lax.fori_loop(0, i, body, 0)

    start = pl.multiple_of(i * tq, tq)
    krow = lax.broadcasted_iota(jnp.int32, (tq, 2 * tq), 0)
    qcol = lax.broadcasted_iota(jnp.int32, (tq, 2 * tq), 1) % tq
    update(k_ref[0, pl.ds(start, tq), :], vt_ref[0, :, pl.ds(start, tq)], krow <= qcol)

    lam = _diff_lambda(lq1, lk1, lq2, lk2)
    o = acc_sc[...] * (1.0 / l_sc[...])
    o_ref[...] = (o[:, :tq] - lam * o[:, tq:]).T


def _prompt_attn(lq1, lk1, lq2, lk2, q, katt, vt, kmatt, vtm, *, tq):
    nh, s, _ = q.shape
    small = lambda a: pl.BlockSpec(a.shape, lambda h, i: (0,) * a.ndim)
    return pl.pallas_call(
        functools.partial(_prompt_attn_kernel, tq=tq),
        grid=(nh, s // tq),
        in_specs=[small(lq1), small(lk1), small(lq2), small(lk2),
                  pl.BlockSpec((1, tq, HEAD_V), lambda h, i: (h, i, 0)),
                  pl.BlockSpec((1, s, HEAD_V), lambda h, i: (h, 0, 0)),
                  pl.BlockSpec((1, HEAD_V, s), lambda h, i: (h, 0, 0)),
                  pl.BlockSpec((1, LANES, HEAD_V), lambda h, i: (h, 0, 0)),
                  pl.BlockSpec((1, HEAD_V, LANES), lambda h, i: (h, 0, 0))],
        out_specs=pl.BlockSpec((tq, HEAD_V), lambda h, i: (i, h)),
        out_shape=jax.ShapeDtypeStruct((s, D_ATT), F32),
        scratch_shapes=[pltpu.VMEM((1, 2 * tq), F32), pltpu.VMEM((1, 2 * tq), F32),
                        pltpu.VMEM((HEAD_V, 2 * tq), F32)],
        compiler_params=pltpu.CompilerParams(dimension_semantics=("arbitrary", "arbitrary"),
                                             vmem_limit_bytes=VMEM_LIMIT),
        name="prompt_attn",
    )(lq1, lk1, lq2, lk2, q, katt, vt, kmatt, vtm)


def _epilogue_kernel(x_ref, o_ref, gatt_ref, ylg_ref, sub_ref, wout_ref, npost_ref, y_ref):
    parts = []
    for h in range(N_ATT_HEADS):
        sl = slice(h * HEAD_V, (h + 1) * HEAD_V)
        o = _rms(o_ref[:, sl], sub_ref[...]) * (1.0 - LAM_INIT)
        parts.append((o * _silu(gatt_ref[:, sl])).astype(BF16))
    parts.append(ylg_ref[...].astype(BF16))
    mix = jnp.dot(jnp.concatenate(parts, axis=1), wout_ref[...], preferred_element_type=F32)
    y_ref[...] = x_ref[...] + _rms(mix, npost_ref[...])


def _epilogue(x, o_att, gatt, ylg, sub, wout, npost, *, tr):
    s, d = x.shape
    full = lambda a: pl.BlockSpec(a.shape, lambda i: (0,) * a.ndim)
    rows = lambda w: pl.BlockSpec((tr, w), lambda i: (i, 0))
    return pl.pallas_call(
        _epilogue_kernel,
        grid=(s // tr,),
        in_specs=[rows(d), rows(D_ATT), rows(D_ATT), rows(D_LRU), full(sub), full(wout), full(npost)],
        out_specs=rows(d),
        out_shape=jax.ShapeDtypeStruct((s, d), F32),
        compiler_params=pltpu.CompilerParams(dimension_semantics=("arbitrary",), vmem_limit_bytes=VMEM_LIMIT),
        name="epilogue",
    )(x, o_att, gatt, ylg, sub, wout, npost)


def _sample_pre_kernel(x_ref, npre_ref, win_ref, inv_ref, convw_ref, convb_ref, wg_ref, br_ref, bi_ref,
                       lam_ref, h0_ref, sc_ref,
                       q_ref, k_ref, v_ref, gatt_ref, ylg_ref, h_ref, cnew_ref, *, past_len):
    rows = x_ref.shape[0]
    u = _rms(x_ref[...], npre_ref[...]).astype(BF16)
    z = jnp.dot(u, win_ref[...], preferred_element_type=F32)
    ang = float(past_len) * inv_ref[...]
    cos, sin = jnp.cos(ang), jnp.sin(ang)
    fh = _first_half_mask((rows, LANES))
    sin_signed = jnp.where(fh[0:1], -sin, sin)
    for h in range(N_ATT_HEADS):
        sl = slice(h * HEAD_V, (h + 1) * HEAD_V)
        q_ref[:, sl] = _rope_block(z[:, sl], cos, sin_signed, fh) * (HEAD_DIM ** -0.5)
        k_ref[:, sl] = _rope_block(z[:, D_ATT + h * HEAD_V:D_ATT + (h + 1) * HEAD_V], cos, sin_signed, fh)
    v_ref[...] = z[:, 2 * D_ATT:3 * D_ATT]
    gatt_ref[...] = z[:, 3 * D_ATT:4 * D_ATT]
    xb = z[:, 4 * D_ATT:4 * D_ATT + D_LRU]
    glru = z[:, 4 * D_ATT + D_LRU:]
    xc = convb_ref[...] + convw_ref[CONV_WIDTH - 1:CONV_WIDTH, :] * xb
    for j in range(CONV_WIDTH - 1):
        xc = xc + convw_ref[j:j + 1, :] * sc_ref[:, j * D_LRU:(j + 1) * D_LRU]
    a, uu = _lru_gates(xc, wg_ref, br_ref, bi_ref, lam_ref)
    hn = a * h0_ref[...] + uu
    h_ref[...] = hn
    ylg_ref[...] = hn * _silu(glru)
    cnew_ref[:, 0:(CONV_WIDTH - 2) * D_LRU] = sc_ref[:, D_LRU:]
    cnew_ref[:, (CONV_WIDTH - 2) * D_LRU:] = xb


def _sample_pre(x, npre, win, inv, convw, convb, wg, br, bi, lam, h0, sc, *, past_len):
    rows = x.shape[0]
    out_shape = tuple(jax.ShapeDtypeStruct((rows, w), F32)
                      for w in (D_ATT, D_ATT, D_ATT, D_ATT, D_LRU, D_LRU, (CONV_WIDTH - 1) * D_LRU))
    return pl.pallas_call(
        functools.partial(_sample_pre_kernel, past_len=past_len),
        out_shape=out_shape,
        compiler_params=pltpu.CompilerParams(vmem_limit_bytes=VMEM_LIMIT),
        name="sample_pre",
    )(x, npre, win, inv, convw, convb, wg, br, bi, lam, h0, sc)


def _decode_attn_kernel(pt_ref, lq1, lk1, lq2, lk2, eseg_ref, q_ref, kn_ref, vn_ref, *rest, pps):
    k_refs, v_refs = rest[:pps], rest[pps:2 * pps]
    o_ref, m_sc, l_sc, acc_sc = rest[2 * pps:]
    j = pl.program_id(1)
    q = q_ref[0]

    def seg_scores(prod):
        hi = prod.astype(BF16)
        lo = (prod - hi.astype(F32)).astype(BF16)
        return (jnp.dot(hi, eseg_ref[...], preferred_element_type=F32)
                + jnp.dot(lo, eseg_ref[...], preferred_element_type=F32))

    def col_form(row):
        return jnp.broadcast_to(row, (LANES, LANES)).T

    @pl.when(j == 0)
    def _():
        m_sc[...] = seg_scores(jnp.broadcast_to(q * kn_ref[0], (SUBLANES, q.shape[1])))[0:1]
        l_sc[...] = jnp.ones_like(l_sc)
        acc_sc[...] = jnp.broadcast_to(vn_ref[0], acc_sc.shape)

    prod = jnp.concatenate([(k_refs[t][...] * q).astype(BF16) for t in range(pps)], axis=0)
    s = jnp.dot(prod, eseg_ref[...], preferred_element_type=F32)
    m_old = m_sc[...]
    m_new = jnp.maximum(m_old, jnp.max(s, axis=0, keepdims=True))
    alpha = jnp.exp(m_old - m_new)
    p = jnp.exp(s - m_new)
    l_sc[...] = alpha * l_sc[...] + jnp.sum(p, axis=0, keepdims=True)
    m_sc[...] = m_new
    vcat = jnp.concatenate([v_refs[t][...].astype(BF16) for t in range(pps)], axis=0)
    pv = jnp.dot(p.T.astype(BF16), vcat, preferred_element_type=F32)
    a_col = col_form(alpha)
    for h in range(N_ATT_HEADS):
        sl = slice(h * HEAD_V, (h + 1) * HEAD_V)
        acc_sc[:, sl] = acc_sc[:, sl] * a_col + pv[:, sl]

    @pl.when(j == pl.num_programs(1) - 1)
    def _():
        lam = _diff_lambda(lq1, lk1, lq2, lk2)
        col = lax.broadcasted_iota(jnp.int32, (1, LANES), 1)
        w_col = col_form(jnp.where(col < N_ATT_HEADS, 1.0, -lam) / l_sc[...])
        out = []
        for h in range(N_ATT_HEADS):
            sl = slice(h * HEAD_V, (h + 1) * HEAD_V)
            a8 = acc_sc[0:SUBLANES, sl] * w_col[0:SUBLANES, :]
            r8 = lax.broadcasted_iota(jnp.int32, a8.shape, 0)
            out.append(jnp.sum(jnp.where(r8 % N_ATT_HEADS == h, a8, 0.0), axis=0, keepdims=True))
        o_ref[0] = jnp.concatenate(out, axis=1)


def _decode_attn(page_table, lq1, lk1, lq2, lk2, eseg, q, kn, vn, cache_k, cache_v, *, pps):
    db, n_pages = page_table.shape
    _, page, width = cache_k.shape
    small = lambda a: pl.BlockSpec(a.shape, lambda b, j, pt: (0,) * a.ndim)
    row = pl.BlockSpec((1, 1, width), lambda b, j, pt: (b, 0, 0))

    def page_spec(t):
        return pl.BlockSpec((None, page, width), lambda b, j, pt: (pt[b * n_pages + j * pps + t], 0, 0))

    grid_spec = pltpu.PrefetchScalarGridSpec(
        num_scalar_prefetch=1,
        grid=(db, n_pages // pps),
        in_specs=[small(lq1), small(lk1), small(lq2), small(lk2), small(eseg), row, row, row]
                 + [page_spec(t) for t in range(pps)] * 2,
        out_specs=row,
        scratch_shapes=[pltpu.VMEM((1, LANES), F32), pltpu.VMEM((1, LANES), F32),
                        pltpu.VMEM((LANES, width), F32)],
    )
    return pl.pallas_call(
        functools.partial(_decode_attn_kernel, pps=pps),
        grid_spec=grid_spec,
        out_shape=jax.ShapeDtypeStruct((db, 1, width), F32),
        compiler_params=pltpu.CompilerParams(dimension_semantics=("arbitrary", "arbitrary"),
                                             vmem_limit_bytes=VMEM_LIMIT),
        name="decode_attn",
    )(page_table.reshape(-1), lq1, lk1, lq2, lk2, eseg, q.reshape(db, 1, width), kn.reshape(db, 1, width),
      vn.reshape(db, 1, width), *([cache_k] * pps), *([cache_v] * pps))


def _tile(n, target):
    t = min(n, target)
    assert n % t == 0, (n, t)
    return t


def kernel(x_prompt, x_sample, cache_k, cache_v, state_h, state_conv, page_table, meta_tokens, norm_pre, w_in, conv_w, conv_b, w_gate_r, b_gate_r, w_gate_i, b_gate_i, lru_lambda, lambda_q1, lambda_k1, lambda_q2, lambda_k2, attn_subnorm, w_out, norm_post):
    assert x_prompt.shape[0] == 1 and x_sample.shape[1] == 1 and w_in.shape[0] == 1
    seq, d_model = x_prompt.shape[1], x_prompt.shape[2]
    db = x_sample.shape[0]
    n_pool, page = cache_k.shape[1], cache_k.shape[2]
    past_len = page_table.shape[1] * page

    win = w_in[0].astype(BF16)
    wout = w_out[0].astype(BF16)
    row = lambda a: a.reshape(1, -1)
    npre, npost, sub = row(norm_pre[0]), row(norm_post[0]), row(attn_subnorm[0])
    convw, convb = conv_w[0], row(conv_b[0])
    br, bi, lam = row(b_gate_r[0]), row(b_gate_i[0]), row(lru_lambda[0])
    blocks = jnp.arange(D_LRU) // (D_LRU // N_LRU_BLOCKS)
    same_block = blocks[:, None] == blocks[None, :]
    bdiag = lambda w: jnp.where(same_block, jnp.tile(w.reshape(D_LRU, -1), (1, N_LRU_BLOCKS)), 0.0)
    wg = jnp.concatenate([bdiag(w_gate_r[0]), bdiag(w_gate_i[0])], axis=1).astype(BF16)
    lq1, lk1, lq2, lk2 = row(lambda_q1[0]), row(lambda_k1[0]), row(lambda_q2[0]), row(lambda_k2[0])
    inv = ROPE_THETA ** (-jnp.arange(0, HEAD_DIM, 2, dtype=F32) / HEAD_DIM)
    inv = jnp.tile(inv, LANES // (HEAD_DIM // 2)).reshape(1, LANES)
    lane = jnp.arange(D_ATT)
    seg = ((lane % HEAD_V) // HEAD_DIM) * N_ATT_HEADS + lane // HEAD_V
    eseg = (seg[:, None] == jnp.arange(LANES)[None, :]).astype(BF16)

    xp = x_prompt[0]
    tr = _tile(seq, 512)
    (q, katt, vt, k_x, v_x, gatt, ylg, kmatt, vtm, kmeta, vmeta, hlast, convlast) = _prompt_pre(
        xp, meta_tokens, npre, win, inv, convw, convb, wg, br, bi, lam, tr=tr)
    o_att = _prompt_attn(lq1, lk1, lq2, lk2, q, katt, vt, kmatt, vtm, tq=_tile(seq, 256))
    y_prompt = _epilogue(xp, o_att, gatt, ylg, sub, wout, npost, tr=tr)

    xs = x_sample[:, 0]
    qs, ks, vs, gatt_s, ylg_s, h_s, conv_s = _sample_pre(
        xs, npre, win, inv, convw, convb, wg, br, bi, lam, state_h[0],
        state_conv[0].reshape(db, (CONV_WIDTH - 1) * D_LRU), past_len=past_len)
    o_s = _decode_attn(page_table, lq1, lk1, lq2, lk2, eseg, qs, ks, vs,
                       cache_k[0].reshape(n_pool, page, D_ATT), cache_v[0].reshape(n_pool, page, D_ATT), pps=8)
    y_sample = _epilogue(xs, o_s.reshape(db, D_ATT), gatt_s, ylg_s, sub, wout, npost, tr=db)

    t_all = seq + N_META
    k_prompt = jnp.concatenate([kmeta[:N_META], k_x], axis=0).reshape(1, 1, t_all, N_ATT_HEADS, HEAD_V)
    v_prompt = jnp.concatenate([vmeta[:N_META], v_x], axis=0).reshape(1, 1, t_all, N_ATT_HEADS, HEAD_V)
    return (y_prompt[None], y_sample[:, None], k_prompt, v_prompt,
            hlast[0:1][None], convlast[SUBLANES - (CONV_WIDTH - 1):][None, None],
            ks.reshape(1, db, 1, N_ATT_HEADS, HEAD_V), vs.reshape(1, db, 1, N_ATT_HEADS, HEAD_V),
            h_s[None], conv_s.reshape(1, db, CONV_WIDTH - 1, D_LRU))
```

```python
import functools
import math

import jax
import jax.numpy as jnp
from jax import lax
from jax.experimental import pallas as pl
from jax.experimental.pallas import tpu as pltpu

N_META = 16
N_ATT_HEADS = 4
HEAD_DIM = 64
HEAD_V = 2 * HEAD_DIM
D_ATT = N_ATT_HEADS * HEAD_V
D_LRU = 512
N_LRU_BLOCKS = 8
CONV_WIDTH = 4
LRU_C = 8.0
ROPE_THETA = 10000.0
EPS = 1e-6
LAM_INIT = 0.8 - 0.6 * math.exp(-0.3 * 0)
Q_SCALE = HEAD_DIM ** -0.5 * math.log2(math.e)

LANES = 128
SUBLANES = 8
VMEM_LIMIT = 56 * 1024 * 1024

F32 = jnp.float32
BF16 = jnp.bfloat16


def _rms(x, g):
    return x * lax.rsqrt(jnp.mean(x * x, axis=-1, keepdims=True) + EPS) * g


def _silu(x):
    return x * jax.nn.sigmoid(x)


def _first_half_mask(shape):
    lane = lax.broadcasted_iota(jnp.int32, shape, len(shape) - 1)
    return (lane % HEAD_DIM) < (HEAD_DIM // 2)


def _rope_block(zb, cos, sin_signed, first_half):
    half = HEAD_DIM // 2
    partner = jnp.where(first_half, pltpu.roll(zb, LANES - half, axis=1), pltpu.roll(zb, half, axis=1))
    return zb * cos + partner * sin_signed


def _diff_lambda(lq1, lk1, lq2, lk2):
    return (jnp.exp(jnp.sum(lq1[...] * lk1[...])) - jnp.exp(jnp.sum(lq2[...] * lk2[...])) + LAM_INIT)


def _lru_gates(xc, wg_ref, br_ref, bi_ref, lam_ref):
    gates = jnp.dot(xc.astype(BF16), wg_ref[...], preferred_element_type=F32)
    r = jax.nn.sigmoid(gates[:, :D_LRU] + br_ref[...])
    i = jax.nn.sigmoid(gates[:, D_LRU:] + bi_ref[...])
    neg_lam = -lam_ref[...]
    softplus = jnp.maximum(neg_lam, 0.0) + jnp.log1p(jnp.exp(-jnp.abs(neg_lam)))
    log_a = -LRU_C * r * softplus
    a = jnp.exp(log_a)
    u = jnp.sqrt(1.0 - jnp.exp(2.0 * log_a)) * (i * xc)
    return a, u


def _prompt_pre_kernel(x_ref, meta_ref, npre_ref, win_ref, inv_ref, convw_ref, convb_ref, wg_ref,
                       br_ref, bi_ref, lam_ref,
                       q_ref, katt_ref, vt_ref, k_ref, v_ref, gatt_ref, ylg_ref,
                       kmatt_ref, vtm_ref, kmeta_ref, vmeta_ref, hlast_ref, convlast_ref,
                       cos_tab, sin_tab, conv_carry, h_carry, *, tr):
    step = pl.program_id(0)

    def rope_tables(pos0, rows):
        base = pos0.astype(F32) * inv_ref[...]
        cb, sb = jnp.cos(base), jnp.sin(base)
        cr, sr = cos_tab[0:rows, :], sin_tab[0:rows, :]
        cos = cr * cb - sr * sb
        sin = sr * cb + cr * sb
        return cos, jnp.where(_first_half_mask((rows, LANES)), -sin, sin)

    def project(x, pos0, rows):
        u = _rms(x, npre_ref[...]).astype(BF16)
        cos, sin_signed = rope_tables(pos0, rows)
        fh = _first_half_mask((rows, LANES))
        zq = jnp.dot(u, win_ref[:, 0:D_ATT], preferred_element_type=F32)
        zk = jnp.dot(u, win_ref[:, D_ATT:2 * D_ATT], preferred_element_type=F32)
        zv = jnp.dot(u, win_ref[:, 2 * D_ATT:3 * D_ATT], preferred_element_type=F32)
        qs, ks = [], []
        for h in range(N_ATT_HEADS):
            sl = slice(h * HEAD_V, (h + 1) * HEAD_V)
            qs.append(_rope_block(zq[:, sl], cos, sin_signed, fh) * Q_SCALE)
            ks.append(_rope_block(zk[:, sl], cos, sin_signed, fh))
        return u, qs, ks, zv

    def lru(xb, glru, rows, store_y):
        xp = jnp.concatenate([conv_carry[...], xb], axis=0)
        off = SUBLANES - (CONV_WIDTH - 1)
        xc = convb_ref[...] + convw_ref[0:1, :] * xp[off:off + rows]
        for j in range(1, CONV_WIDTH):
            xc = xc + convw_ref[j:j + 1, :] * xp[off + j:off + j + rows]
        conv_carry[...] = xp[rows:rows + SUBLANES]
        a, u = _lru_gates(xc, wg_ref, br_ref, bi_ref, lam_ref)
        row = lax.broadcasted_iota(jnp.int32, (rows, D_LRU), 0) % SUBLANES
        d = 1
        while d < SUBLANES:
            valid = row >= d
            u = jnp.where(valid, a * pltpu.roll(u, d, axis=0) + u, u)
            a = jnp.where(valid, a * pltpu.roll(a, d, axis=0), a)
            d *= 2
        h = h_carry[...]
        for g in range(rows // SUBLANES):
            sl = slice(g * SUBLANES, (g + 1) * SUBLANES)
            hg = a[sl] * h + u[sl]
            if store_y:
                ylg_ref[sl, :] = hg * _silu(glru[sl])
            h = jnp.broadcast_to(hg[SUBLANES - 1:SUBLANES, :], (SUBLANES, D_LRU))
        h_carry[...] = h

    @pl.when(step == 0)
    def _():
        r = lax.broadcasted_iota(jnp.int32, (tr, LANES), 0).astype(F32)
        ang = r * inv_ref[...]
        cos_tab[...] = jnp.cos(ang)
        sin_tab[...] = jnp.sin(ang)
        conv_carry[...] = jnp.zeros_like(conv_carry)
        h_carry[...] = jnp.zeros_like(h_carry)
        xm = jnp.concatenate([meta_ref[...], jnp.zeros((LANES - N_META, meta_ref.shape[1]), F32)], axis=0)
        u, _, ks, zv = project(xm, jnp.zeros((), jnp.int32), LANES)
        for h in range(N_ATT_HEADS):
            sl = slice(h * HEAD_V, (h + 1) * HEAD_V)
            kmeta_ref[:, sl] = ks[h]
            kmatt_ref[h] = ks[h].astype(BF16)
            vtm_ref[h] = zv[:, sl].T.astype(BF16)
        vmeta_ref[...] = zv
        um = u[0:N_META]
        xb = jnp.dot(um, win_ref[:, 4 * D_ATT:4 * D_ATT + D_LRU], preferred_element_type=F32)
        lru(xb, None, N_META, False)

    u, qs, ks, zv = project(x_ref[...], N_META + step * tr, tr)
    for h in range(N_ATT_HEADS):
        sl = slice(h * HEAD_V, (h + 1) * HEAD_V)
        q_ref[h] = qs[h].astype(BF16)
        k_ref[:, sl] = ks[h]
        katt_ref[h] = ks[h].astype(BF16)
        vt_ref[h] = zv[:, sl].T.astype(BF16)
    v_ref[...] = zv
    gatt_ref[...] = jnp.dot(u, win_ref[:, 3 * D_ATT:4 * D_ATT], preferred_element_type=F32)
    xb = jnp.dot(u, win_ref[:, 4 * D_ATT:4 * D_ATT + D_LRU], preferred_element_type=F32)
    glru = jnp.dot(u, win_ref[:, 4 * D_ATT + D_LRU:], preferred_element_type=F32)
    lru(xb, glru, tr, True)
    hlast_ref[...] = h_carry[...]
    convlast_ref[...] = conv_carry[...]


def _prompt_pre(x, meta, npre, win, inv, convw, convb, wg, br, bi, lam, *, tr):
    s, d = x.shape
    nt = s // tr
    full = lambda a: pl.BlockSpec(a.shape, lambda i: (0,) * a.ndim)
    out_shape = (
        jax.ShapeDtypeStruct((N_ATT_HEADS, s, HEAD_V), BF16),
        jax.ShapeDtypeStruct((N_ATT_HEADS, s, HEAD_V), BF16),
        jax.ShapeDtypeStruct((N_ATT_HEADS, HEAD_V, s), BF16),
        jax.ShapeDtypeStruct((s, D_ATT), F32),
        jax.ShapeDtypeStruct((s, D_ATT), F32),
        jax.ShapeDtypeStruct((s, D_ATT), F32),
        jax.ShapeDtypeStruct((s, D_LRU), F32),
        jax.ShapeDtypeStruct((N_ATT_HEADS, LANES, HEAD_V), BF16),
        jax.ShapeDtypeStruct((N_ATT_HEADS, HEAD_V, LANES), BF16),
        jax.ShapeDtypeStruct((LANES, D_ATT), F32),
        jax.ShapeDtypeStruct((LANES, D_ATT), F32),
        jax.ShapeDtypeStruct((SUBLANES, D_LRU), F32),
        jax.ShapeDtypeStruct((SUBLANES, D_LRU), F32),
    )
    out_specs = (
        pl.BlockSpec((N_ATT_HEADS, tr, HEAD_V), lambda i: (0, i, 0)),
        pl.BlockSpec((N_ATT_HEADS, tr, HEAD_V), lambda i: (0, i, 0)),
        pl.BlockSpec((N_ATT_HEADS, HEAD_V, tr), lambda i: (0, 0, i)),
        pl.BlockSpec((tr, D_ATT), lambda i: (i, 0)),
        pl.BlockSpec((tr, D_ATT), lambda i: (i, 0)),
        pl.BlockSpec((tr, D_ATT), lambda i: (i, 0)),
        pl.BlockSpec((tr, D_LRU), lambda i: (i, 0)),
    ) + tuple(pl.BlockSpec(o.shape, lambda i, n=len(o.shape): (0,) * n) for o in out_shape[7:])
    return pl.pallas_call(
        functools.partial(_prompt_pre_kernel, tr=tr),
        grid=(nt,),
        in_specs=[pl.BlockSpec((tr, d), lambda i: (i, 0))] + [full(a) for a in (meta, npre, win, inv, convw, convb, wg, br, bi, lam)],
        out_specs=out_specs,
        out_shape=out_shape,
        scratch_shapes=[pltpu.VMEM((tr, LANES), F32), pltpu.VMEM((tr, LANES), F32),
                        pltpu.VMEM((SUBLANES, D_LRU), F32), pltpu.VMEM((SUBLANES, D_LRU), F32)],
        compiler_params=pltpu.CompilerParams(dimension_semantics=("arbitrary",), vmem_limit_bytes=VMEM_LIMIT),
        name="prompt_pre",
    )(x, meta, npre, win, inv, convw, convb, wg, br, bi, lam)


def _prompt_attn_kernel(lq1, lk1, lq2, lk2, q_ref, k_ref, vt_ref, kmeta_ref, vtmeta_ref, o_ref,
                        m_sc, l_sc, acc_sc, sa_sc, sb_sc, *, tq, tk):
    i = pl.program_id(1)
    n_unmasked = (i * tq) // tk
    qt = q_ref[0].astype(F32).T
    rows = lax.broadcasted_iota(jnp.int32, qt.shape, 0)
    qb = jnp.concatenate([jnp.where(rows < HEAD_DIM, qt, 0.0), jnp.where(rows >= HEAD_DIM, qt, 0.0)],
                         axis=1).astype(BF16)

    def scores(c):
        start = pl.multiple_of(c * tk, tk)
        return jnp.dot(k_ref[0, pl.ds(start, tk), :], qb, preferred_element_type=F32)

    def v_chunk(c):
        return vt_ref[0, :, pl.ds(pl.multiple_of(c * tk, tk), tk)]

    def update(s, vtt, mask):
        if mask is not None:
            s = jnp.where(mask, s, -jnp.inf)
        m_old = m_sc[...]
        m_new = jnp.maximum(m_old, jnp.max(s, axis=0, keepdims=True))
        alpha = jnp.exp2(m_old - m_new)
        p = jnp.exp2(s - m_new)
        l_sc[...] = alpha * l_sc[...] + jnp.sum(p, axis=0, keepdims=True)
        acc_sc[...] = alpha * acc_sc[...] + jnp.dot(vtt, p.astype(BF16), preferred_element_type=F32)
        m_sc[...] = m_new

    m_sc[...] = jnp.full_like(m_sc, -jnp.inf)
    l_sc[...] = jnp.zeros_like(l_sc)
    acc_sc[...] = jnp.zeros_like(acc_sc)

    krow = lax.broadcasted_iota(jnp.int32, (LANES, 2 * tq), 0)
    update(jnp.dot(kmeta_ref[0], qb, preferred_element_type=F32), vtmeta_ref[0], krow < N_META)

    sa_sc[...] = scores(0)

    def pair(t, carry):
        c = 2 * t
        sb_sc[...] = scores(c + 1)
        update(sa_sc[...], v_chunk(c), None)
        sa_sc[...] = scores(c + 2)
        update(sb_sc[...], v_chunk(c + 1), None)
        return carry

    lax.fori_loop(0, n_unmasked // 2, pair, 0)

    kpos = n_unmasked * tk + lax.broadcasted_iota(jnp.int32, (tk, 2 * tq), 0)
    qpos = i * tq + lax.broadcasted_iota(jnp.int32, (tk, 2 * tq), 1) % tq
    causal = kpos <= qpos

    @pl.when(n_unmasked % 2 == 1)
    def _():
        sb_sc[...] = scores(n_unmasked)
        update(sa_sc[...], v_chunk(n_unmasked - 1), None)
        update(sb_sc[...], v_chunk(n_unmasked), causal)

    @pl.when(n_unmasked % 2 == 0)
    def _():
        update(sa_sc[...], v_chunk(n_unmasked), causal)

    lam = _diff_lambda(lq1, lk1, lq2, lk2)
    o = acc_sc[...] * (1.0 / l_sc[...])
    o_ref[...] = (o[:, :tq] - lam * o[:, tq:]).T


def _prompt_attn(lq1, lk1, lq2, lk2, q, katt, vt, kmatt, vtm, *, tq, tk):
    nh, s, _ = q.shape
    small = lambda a: pl.BlockSpec(a.shape, lambda h, i: (0,) * a.ndim)
    return pl.pallas_call(
        functools.partial(_prompt_attn_kernel, tq=tq, tk=tk),
        grid=(nh, s // tq),
        in_specs=[small(lq1), small(lk1), small(lq2), small(lk2),
                  pl.BlockSpec((1, tq, HEAD_V), lambda h, i: (h, i, 0)),
                  pl.BlockSpec((1, s, HEAD_V), lambda h, i: (h, 0, 0)),
                  pl.BlockSpec((1, HEAD_V, s), lambda h, i: (h, 0, 0)),
                  pl.BlockSpec((1, LANES, HEAD_V), lambda h, i: (h, 0, 0)),
                  pl.BlockSpec((1, HEAD_V, LANES), lambda h, i: (h, 0, 0))],
        out_specs=pl.BlockSpec((tq, HEAD_V), lambda h, i: (i, h)),
        out_shape=jax.ShapeDtypeStruct((s, D_ATT), F32),
        scratch_shapes=[pltpu.VMEM((1, 2 * tq), F32), pltpu.VMEM((1, 2 * tq), F32),
                        pltpu.VMEM((HEAD_V, 2 * tq), F32),
                        pltpu.VMEM((tk, 2 * tq), F32), pltpu.VMEM((tk, 2 * tq), F32)],
        compiler_params=pltpu.CompilerParams(dimension_semantics=("arbitrary", "arbitrary"),
                                             vmem_limit_bytes=VMEM_LIMIT),
        name="prompt_attn",
    )(lq1, lk1, lq2, lk2, q, katt, vt, kmatt, vtm)


def _epilogue_kernel(x_ref, o_ref, gatt_ref, ylg_ref, sub_ref, wout_ref, npost_ref, y_ref):
    parts = []
    for h in range(N_ATT_HEADS):
        sl = slice(h * HEAD_V, (h + 1) * HEAD_V)
        o = _rms(o_ref[:, sl], sub_ref[...]) * (1.0 - LAM_INIT)
        parts.append((o * _silu(gatt_ref[:, sl])).astype(BF16))
    parts.append(ylg_ref[...].astype(BF16))
    mix = jnp.dot(jnp.concatenate(parts, axis=1), wout_ref[...], preferred_element_type=F32)
    y_ref[...] = x_ref[...] + _rms(mix, npost_ref[...])


def _epilogue(x, o_att, gatt, ylg, sub, wout, npost, *, tr):
    s, d = x.shape
    full = lambda a: pl.BlockSpec(a.shape, lambda i: (0,) * a.ndim)
    rows = lambda w: pl.BlockSpec((tr, w), lambda i: (i, 0))
    return pl.pallas_call(
        _epilogue_kernel,
        grid=(s // tr,),
        in_specs=[rows(d), rows(D_ATT), rows(D_ATT), rows(D_LRU), full(sub), full(wout), full(npost)],
        out_specs=rows(d),
        out_shape=jax.ShapeDtypeStruct((s, d), F32),
        compiler_params=pltpu.CompilerParams(dimension_semantics=("arbitrary",), vmem_limit_bytes=VMEM_LIMIT),
        name="epilogue",
    )(x, o_att, gatt, ylg, sub, wout, npost)


def _sample_pre_kernel(x_ref, npre_ref, win_ref, inv_ref, convw_ref, convb_ref, wg_ref, br_ref, bi_ref,
                       lam_ref, h0_ref, sc_ref,
                       q_ref, k_ref, v_ref, gatt_ref, ylg_ref, h_ref, cnew_ref, *, past_len):
    rows = x_ref.shape[0]
    u = _rms(x_ref[...], npre_ref[...]).astype(BF16)
    z = jnp.dot(u, win_ref[...], preferred_element_type=F32)
    ang = float(past_len) * inv_ref[...]
    cos, sin = jnp.cos(ang), jnp.sin(ang)
    fh = _first_half_mask((rows, LANES))
    sin_signed = jnp.where(fh[0:1], -sin, sin)
    for h in range(N_ATT_HEADS):
        sl = slice(h * HEAD_V, (h + 1) * HEAD_V)
        q_ref[:, sl] = _rope_block(z[:, sl], cos, sin_signed, fh) * Q_SCALE
        k_ref[:, sl] = _rope_block(z[:, D_ATT + h * HEAD_V:D_ATT + (h + 1) * HEAD_V], cos, sin_signed, fh)
    v_ref[...] = z[:, 2 * D_ATT:3 * D_ATT]
    gatt_ref[...] = z[:, 3 * D_ATT:4 * D_ATT]
    xb = z[:, 4 * D_ATT:4 * D_ATT + D_LRU]
    glru = z[:, 4 * D_ATT + D_LRU:]
    xc = convb_ref[...] + convw_ref[CONV_WIDTH - 1:CONV_WIDTH, :] * xb
    for j in range(CONV_WIDTH - 1):
        xc = xc + convw_ref[j:j + 1, :] * sc_ref[:, j * D_LRU:(j + 1) * D_LRU]
    a, uu = _lru_gates(xc, wg_ref, br_ref, bi_ref, lam_ref)
    hn = a * h0_ref[...] + uu
    h_ref[...] = hn
    ylg_ref[...] = hn * _silu(glru)
    cnew_ref[:, 0:(CONV_WIDTH - 2) * D_LRU] = sc_ref[:, D_LRU:]
    cnew_ref[:, (CONV_WIDTH - 2) * D_LRU:] = xb


def _sample_pre(x, npre, win, inv, convw, convb, wg, br, bi, lam, h0, sc, *, past_len):
    rows = x.shape[0]
    out_shape = tuple(jax.ShapeDtypeStruct((rows, w), F32)
                      for w in (D_ATT, D_ATT, D_ATT, D_ATT, D_LRU, D_LRU, (CONV_WIDTH - 1) * D_LRU))
    return pl.pallas_call(
        functools.partial(_sample_pre_kernel, past_len=past_len),
        out_shape=out_shape,
        compiler_params=pltpu.CompilerParams(vmem_limit_bytes=VMEM_LIMIT),
        name="sample_pre",
    )(x, npre, win, inv, convw, convb, wg, br, bi, lam, h0, sc)


def _decode_attn_kernel(pt_ref, lq1, lk1, lq2, lk2, q_ref, kn_ref, vn_ref, *rest, pps):
    k_refs, v_refs = rest[:pps], rest[pps:2 * pps]
    o_ref, qrows_sc, m_sc, l_sc, acc_sc = rest[2 * pps:]
    j = pl.program_id(1)
    nh = N_ATT_HEADS

    @pl.when(j == 0)
    def _():
        q4 = q_ref[0]
        lane = lax.broadcasted_iota(jnp.int32, q4.shape, 1)
        q1, q2 = jnp.where(lane < HEAD_DIM, q4, 0.0), jnp.where(lane >= HEAD_DIM, q4, 0.0)
        qrows_sc[...] = jnp.concatenate([q1, q2], axis=0).astype(BF16)
        kn = kn_ref[0]
        m0 = jnp.concatenate([jnp.sum(q1 * kn, axis=1, keepdims=True), jnp.sum(q2 * kn, axis=1, keepdims=True)], axis=0)
        m_sc[...] = jnp.broadcast_to(m0, m_sc.shape)
        l_sc[...] = jnp.ones_like(l_sc)
        acc_sc[...] = jnp.concatenate([vn_ref[0], vn_ref[0]], axis=0)

    kcat = jnp.concatenate([k_refs[t][...].astype(BF16) for t in range(pps)], axis=0)
    s = lax.dot_general(qrows_sc[...], kcat, (((1,), (1,)), ((), ())), preferred_element_type=F32)
    head_of_col = lax.broadcasted_iota(jnp.int32, s.shape, 1) % nh
    head_of_row = lax.broadcasted_iota(jnp.int32, s.shape, 0) % nh
    s = jnp.where(head_of_col == head_of_row, s, -jnp.inf)
    m_old = m_sc[:, 0:1]
    m_new = jnp.maximum(m_old, jnp.max(s, axis=1, keepdims=True))
    alpha = jnp.exp2(m_old - m_new)
    p = jnp.exp2(s - m_new)
    l_sc[...] = alpha * l_sc[...] + jnp.sum(p, axis=1, keepdims=True)
    m_sc[...] = jnp.broadcast_to(m_new, m_sc.shape)
    vcat = jnp.concatenate([v_refs[t][...].astype(BF16) for t in range(pps)], axis=0)
    acc_sc[...] = alpha * acc_sc[...] + jnp.dot(p.astype(BF16), vcat, preferred_element_type=F32)

    @pl.when(j == pl.num_programs(1) - 1)
    def _():
        lam = _diff_lambda(lq1, lk1, lq2, lk2)
        o = acc_sc[...] / l_sc[...]
        o_ref[0] = o[0:nh] - lam * o[nh:2 * nh]


def _decode_attn(page_table, lq1, lk1, lq2, lk2, q, kn, vn, cache_k, cache_v, *, pps):
    db, n_pages = page_table.shape
    _, prow, width = cache_k.shape
    small = lambda a: pl.BlockSpec(a.shape, lambda b, j, pt: (0,) * a.ndim)
    row = pl.BlockSpec((1, N_ATT_HEADS, width), lambda b, j, pt: (b, 0, 0))

    def page_spec(t):
        return pl.BlockSpec((None, prow, width), lambda b, j, pt: (pt[b * n_pages + j * pps + t], 0, 0))

    grid_spec = pltpu.PrefetchScalarGridSpec(
        num_scalar_prefetch=1,
        grid=(db, n_pages // pps),
        in_specs=[small(lq1), small(lk1), small(lq2), small(lk2), row, row, row]
                 + [page_spec(t) for t in range(pps)] * 2,
        out_specs=row,
        scratch_shapes=[pltpu.VMEM((2 * N_ATT_HEADS, width), BF16), pltpu.VMEM((2 * N_ATT_HEADS, LANES), F32),
                        pltpu.VMEM((2 * N_ATT_HEADS, LANES), F32), pltpu.VMEM((2 * N_ATT_HEADS, width), F32)],
    )
    return pl.pallas_call(
        functools.partial(_decode_attn_kernel, pps=pps),
        grid_spec=grid_spec,
        out_shape=jax.ShapeDtypeStruct((db, N_ATT_HEADS, width), F32),
        compiler_params=pltpu.CompilerParams(dimension_semantics=("arbitrary", "arbitrary"),
                                             vmem_limit_bytes=VMEM_LIMIT),
        name="decode_attn",
    )(page_table.reshape(-1), lq1, lk1, lq2, lk2, q, kn, vn, *([cache_k] * pps), *([cache_v] * pps))


def _tile(n, target):
    t = min(n, target)
    assert n % t == 0, (n, t)
    return t


def kernel(x_prompt, x_sample, cache_k, cache_v, state_h, state_conv, page_table, meta_tokens, norm_pre, w_in, conv_w, conv_b, w_gate_r, b_gate_r, w_gate_i, b_gate_i, lru_lambda, lambda_q1, lambda_k1, lambda_q2, lambda_k2, attn_subnorm, w_out, norm_post):
    assert x_prompt.shape[0] == 1 and x_sample.shape[1] == 1 and w_in.shape[0] == 1
    seq, d_model = x_prompt.shape[1], x_prompt.shape[2]
    db = x_sample.shape[0]
    n_pool, page = cache_k.shape[1], cache_k.shape[2]
    past_len = page_table.shape[1] * page

    win = w_in[0].astype(BF16)
    wout = w_out[0].astype(BF16)
    row = lambda a: a.reshape(1, -1)
    npre, npost, sub = row(norm_pre[0]), row(norm_post[0]), row(attn_subnorm[0])
    convw, convb = conv_w[0], row(conv_b[0])
    br, bi, lam = row(b_gate_r[0]), row(b_gate_i[0]), row(lru_lambda[0])
    blocks = jnp.arange(D_LRU) // (D_LRU // N_LRU_BLOCKS)
    same_block = blocks[:, None] == blocks[None, :]
    bdiag = lambda w: jnp.where(same_block, jnp.tile(w.reshape(D_LRU, -1), (1, N_LRU_BLOCKS)), 0.0)
    wg = jnp.concatenate([bdiag(w_gate_r[0]), bdiag(w_gate_i[0])], axis=1).astype(BF16)
    lq1, lk1, lq2, lk2 = row(lambda_q1[0]), row(lambda_k1[0]), row(lambda_q2[0]), row(lambda_k2[0])
    inv = ROPE_THETA ** (-jnp.arange(0, HEAD_DIM, 2, dtype=F32) / HEAD_DIM)
    inv = jnp.tile(inv, LANES // (HEAD_DIM // 2)).reshape(1, LANES)

    xp = x_prompt[0]
    tr = _tile(seq, 512)
    (q, katt, vt, k_x, v_x, gatt, ylg, kmatt, vtm, kmeta, vmeta, hlast, convlast) = _prompt_pre(
        xp, meta_tokens, npre, win, inv, convw, convb, wg, br, bi, lam, tr=tr)
    o_att = _prompt_attn(lq1, lk1, lq2, lk2, q, katt, vt, kmatt, vtm, tq=_tile(seq, 256), tk=_tile(seq, 1024))
    y_prompt = _epilogue(xp, o_att, gatt, ylg, sub, wout, npost, tr=tr)

    xs = x_sample[:, 0]
    qs, ks, vs, gatt_s, ylg_s, h_s, conv_s = _sample_pre(
        xs, npre, win, inv, convw, convb, wg, br, bi, lam, state_h[0],
        state_conv[0].reshape(db, (CONV_WIDTH - 1) * D_LRU), past_len=past_len)
    per_head = lambda a: a.reshape(db, N_ATT_HEADS, HEAD_V)
    o_s = _decode_attn(page_table, lq1, lk1, lq2, lk2, per_head(qs), per_head(ks), per_head(vs),
                       cache_k[0].reshape(n_pool, page * N_ATT_HEADS, HEAD_V),
                       cache_v[0].reshape(n_pool, page * N_ATT_HEADS, HEAD_V), pps=16)
    y_sample = _epilogue(xs, o_s.reshape(db, D_ATT), gatt_s, ylg_s, sub, wout, npost, tr=db)

    t_all = seq + N_META
    k_prompt = jnp.concatenate([kmeta[:N_META], k_x], axis=0).reshape(1, 1, t_all, N_ATT_HEADS, HEAD_V)
    v_prompt = jnp.concatenate([vmeta[:N_META], v_x], axis=0).reshape(1, 1, t_all, N_ATT_HEADS, HEAD_V)
    return (y_prompt[None], y_sample[:, None], k_prompt, v_prompt,
            hlast[0:1][None], convlast[SUBLANES - (CONV_WIDTH - 1):][None, None],
            ks.reshape(1, db, 1, N_ATT_HEADS, HEAD_V), vs.reshape(1, db, 1, N_ATT_HEADS, HEAD_V),
            h_s[None], conv_s.reshape(1, db, CONV_WIDTH - 1, D_LRU))
```

```python
import functools
import math

import jax
import jax.numpy as jnp
from jax import lax
from jax.experimental import pallas as pl
from jax.experimental.pallas import tpu as pltpu

N_META = 16
N_ATT_HEADS = 4
HEAD_DIM = 64
HEAD_V = 2 * HEAD_DIM
D_ATT = N_ATT_HEADS * HEAD_V
D_LRU = 512
N_LRU_BLOCKS = 8
CONV_WIDTH = 4
LRU_C = 8.0
ROPE_THETA = 10000.0
EPS = 1e-6
LAM_INIT = 0.8 - 0.6 * math.exp(-0.3 * 0)
Q_SCALE = HEAD_DIM ** -0.5 * math.log2(math.e)

LANES = 128
SUBLANES = 8
VMEM_LIMIT = 56 * 1024 * 1024

F32 = jnp.float32
BF16 = jnp.bfloat16


def _rms(x, g):
    return x * lax.rsqrt(jnp.mean(x * x, axis=-1, keepdims=True) + EPS) * g


def _silu(x):
    return x * jax.nn.sigmoid(x)


def _first_half_mask(shape):
    lane = lax.broadcasted_iota(jnp.int32, shape, len(shape) - 1)
    return (lane % HEAD_DIM) < (HEAD_DIM // 2)


def _rope_block(zb, cos, sin_signed, first_half):
    half = HEAD_DIM // 2
    partner = jnp.where(first_half, pltpu.roll(zb, LANES - half, axis=1), pltpu.roll(zb, half, axis=1))
    return zb * cos + partner * sin_signed


def _diff_lambda(lq1, lk1, lq2, lk2):
    return (jnp.exp(jnp.sum(lq1[...] * lk1[...])) - jnp.exp(jnp.sum(lq2[...] * lk2[...])) + LAM_INIT)


def _lru_gates(xc, wg_ref, br_ref, bi_ref, lam_ref):
    gates = jnp.dot(xc.astype(BF16), wg_ref[...], preferred_element_type=F32)
    r = jax.nn.sigmoid(gates[:, :D_LRU] + br_ref[...])
    i = jax.nn.sigmoid(gates[:, D_LRU:] + bi_ref[...])
    neg_lam = -lam_ref[...]
    softplus = jnp.maximum(neg_lam, 0.0) + jnp.log1p(jnp.exp(-jnp.abs(neg_lam)))
    log_a = -LRU_C * r * softplus
    a = jnp.exp(log_a)
    u = jnp.sqrt(1.0 - jnp.exp(2.0 * log_a)) * (i * xc)
    return a, u


def _prompt_pre_kernel(x_ref, meta_ref, npre_ref, win_ref, inv_ref, convw_ref, convb_ref, wg_ref,
                       br_ref, bi_ref, lam_ref,
                       q_ref, katt_ref, vt_ref, k_ref, v_ref, gatt_ref, ylg_ref,
                       kmatt_ref, vtm_ref, kmeta_ref, vmeta_ref, hlast_ref, convlast_ref,
                       cos_tab, sin_tab, conv_carry, h_carry, *, tr):
    step = pl.program_id(0)

    def rope_tables(pos0, rows):
        base = pos0.astype(F32) * inv_ref[...]
        cb, sb = jnp.cos(base), jnp.sin(base)
        cr, sr = cos_tab[0:rows, :], sin_tab[0:rows, :]
        cos = cr * cb - sr * sb
        sin = sr * cb + cr * sb
        return cos, jnp.where(_first_half_mask((rows, LANES)), -sin, sin)

    def project(x, pos0, rows):
        u = _rms(x, npre_ref[...]).astype(BF16)
        cos, sin_signed = rope_tables(pos0, rows)
        fh = _first_half_mask((rows, LANES))
        zq = jnp.dot(u, win_ref[:, 0:D_ATT], preferred_element_type=F32)
        zk = jnp.dot(u, win_ref[:, D_ATT:2 * D_ATT], preferred_element_type=F32)
        zv = jnp.dot(u, win_ref[:, 2 * D_ATT:3 * D_ATT], preferred_element_type=F32)
        qs, ks = [], []
        for h in range(N_ATT_HEADS):
            sl = slice(h * HEAD_V, (h + 1) * HEAD_V)
            qs.append(_rope_block(zq[:, sl], cos, sin_signed, fh) * Q_SCALE)
            ks.append(_rope_block(zk[:, sl], cos, sin_signed, fh))
        return u, qs, ks, zv

    def lru(xb, glru, rows, store_y):
        xp = jnp.concatenate([conv_carry[...], xb], axis=0)
        off = SUBLANES - (CONV_WIDTH - 1)
        xc = convb_ref[...] + convw_ref[0:1, :] * xp[off:off + rows]
        for j in range(1, CONV_WIDTH):
            xc = xc + convw_ref[j:j + 1, :] * xp[off + j:off + j + rows]
        conv_carry[...] = xp[rows:rows + SUBLANES]
        a, u = _lru_gates(xc, wg_ref, br_ref, bi_ref, lam_ref)
        row = lax.broadcasted_iota(jnp.int32, (rows, D_LRU), 0) % SUBLANES
        d = 1
        while d < SUBLANES:
            valid = row >= d
            u = jnp.where(valid, a * pltpu.roll(u, d, axis=0) + u, u)
            a = jnp.where(valid, a * pltpu.roll(a, d, axis=0), a)
            d *= 2
        h = h_carry[...]
        for g in range(rows // SUBLANES):
            sl = slice(g * SUBLANES, (g + 1) * SUBLANES)
            hg = a[sl] * h + u[sl]
            if store_y:
                ylg_ref[sl, :] = hg * _silu(glru[sl])
            h = jnp.broadcast_to(hg[SUBLANES - 1:SUBLANES, :], (SUBLANES, D_LRU))
        h_carry[...] = h

    @pl.when(step == 0)
    def _():
        r = lax.broadcasted_iota(jnp.int32, (tr, LANES), 0).astype(F32)
        ang = r * inv_ref[...]
        cos_tab[...] = jnp.cos(ang)
        sin_tab[...] = jnp.sin(ang)
        conv_carry[...] = jnp.zeros_like(conv_carry)
        h_carry[...] = jnp.zeros_like(h_carry)
        xm = jnp.concatenate([meta_ref[...], jnp.zeros((LANES - N_META, meta_ref.shape[1]), F32)], axis=0)
        u, _, ks, zv = project(xm, jnp.zeros((), jnp.int32), LANES)
        for h in range(N_ATT_HEADS):
            sl = slice(h * HEAD_V, (h + 1) * HEAD_V)
            kmeta_ref[:, sl] = ks[h]
            kmatt_ref[h] = ks[h].astype(BF16)
            vtm_ref[h] = zv[:, sl].T.astype(BF16)
        vmeta_ref[...] = zv
        um = u[0:N_META]
        xb = jnp.dot(um, win_ref[:, 4 * D_ATT:4 * D_ATT + D_LRU], preferred_element_type=F32)
        lru(xb, None, N_META, False)

    u, qs, ks, zv = project(x_ref[...], N_META + step * tr, tr)
    for h in range(N_ATT_HEADS):
        sl = slice(h * HEAD_V, (h + 1) * HEAD_V)
        q_ref[h] = qs[h].astype(BF16)
        k_ref[:, sl] = ks[h]
        katt_ref[h] = ks[h].astype(BF16)
        vt_ref[h] = zv[:, sl].T.astype(BF16)
    v_ref[...] = zv
    gatt_ref[...] = jnp.dot(u, win_ref[:, 3 * D_ATT:4 * D_ATT], preferred_element_type=F32)
    xb = jnp.dot(u, win_ref[:, 4 * D_ATT:4 * D_ATT + D_LRU], preferred_element_type=F32)
    glru = jnp.dot(u, win_ref[:, 4 * D_ATT + D_LRU:], preferred_element_type=F32)
    lru(xb, glru, tr, True)
    hlast_ref[...] = h_carry[...]
    convlast_ref[...] = conv_carry[...]


def _prompt_pre(x, meta, npre, win, inv, convw, convb, wg, br, bi, lam, *, tr):
    s, d = x.shape
    nt = s // tr
    full = lambda a: pl.BlockSpec(a.shape, lambda i: (0,) * a.ndim)
    out_shape = (
        jax.ShapeDtypeStruct((N_ATT_HEADS, s, HEAD_V), BF16),
        jax.ShapeDtypeStruct((N_ATT_HEADS, s, HEAD_V), BF16),
        jax.ShapeDtypeStruct((N_ATT_HEADS, HEAD_V, s), BF16),
        jax.ShapeDtypeStruct((s, D_ATT), F32),
        jax.ShapeDtypeStruct((s, D_ATT), F32),
        jax.ShapeDtypeStruct((s, D_ATT), F32),
        jax.ShapeDtypeStruct((s, D_LRU), F32),
        jax.ShapeDtypeStruct((N_ATT_HEADS, LANES, HEAD_V), BF16),
        jax.ShapeDtypeStruct((N_ATT_HEADS, HEAD_V, LANES), BF16),
        jax.ShapeDtypeStruct((LANES, D_ATT), F32),
        jax.ShapeDtypeStruct((LANES, D_ATT), F32),
        jax.ShapeDtypeStruct((SUBLANES, D_LRU), F32),
        jax.ShapeDtypeStruct((SUBLANES, D_LRU), F32),
    )
    out_specs = (
        pl.BlockSpec((N_ATT_HEADS, tr, HEAD_V), lambda i: (0, i, 0)),
        pl.BlockSpec((N_ATT_HEADS, tr, HEAD_V), lambda i: (0, i, 0)),
        pl.BlockSpec((N_ATT_HEADS, HEAD_V, tr), lambda i: (0, 0, i)),
        pl.BlockSpec((tr, D_ATT), lambda i: (i, 0)),
        pl.BlockSpec((tr, D_ATT), lambda i: (i, 0)),
        pl.BlockSpec((tr, D_ATT), lambda i: (i, 0)),
        pl.BlockSpec((tr, D_LRU), lambda i: (i, 0)),
    ) + tuple(pl.BlockSpec(o.shape, lambda i, n=len(o.shape): (0,) * n) for o in out_shape[7:])
    return pl.pallas_call(
        functools.partial(_prompt_pre_kernel, tr=tr),
        grid=(nt,),
        in_specs=[pl.BlockSpec((tr, d), lambda i: (i, 0))] + [full(a) for a in (meta, npre, win, inv, convw, convb, wg, br, bi, lam)],
        out_specs=out_specs,
        out_shape=out_shape,
        scratch_shapes=[pltpu.VMEM((tr, LANES), F32), pltpu.VMEM((tr, LANES), F32),
                        pltpu.VMEM((SUBLANES, D_LRU), F32), pltpu.VMEM((SUBLANES, D_LRU), F32)],
        compiler_params=pltpu.CompilerParams(dimension_semantics=("arbitrary",), vmem_limit_bytes=VMEM_LIMIT),
        name="prompt_pre",
    )(x, meta, npre, win, inv, convw, convb, wg, br, bi, lam)


DECODE_PAGES_PER_SLICE = 4
DECODE_RING = 16


def _n_chunks(i, tq, tk):
    return (i * tq) // tk + 1


def _attn_kernel(pt_ref, lq1, lk1, lq2, lk2, q_ref, k_ref, vt_ref, kmeta_ref, vtmeta_ref,
                 qs_ref, kn_ref, vn_ref, ck_hbm, cv_hbm, o_ref, os_ref,
                 m_sc, l_sc, acc_sc, sa_sc, sb_sc, kring, vring, sems, g_sc, dm_sc, dl_sc, dacc_sc,
                 *, tq, tk, n_pages, total_pages):
    h, i = pl.program_id(0), pl.program_id(1)
    pp, ring, nh = DECODE_PAGES_PER_SLICE, DECODE_RING, N_ATT_HEADS
    n_unmasked = _n_chunks(i, tq, tk) - 1
    lam = _diff_lambda(lq1, lk1, lq2, lk2)

    def page_copies(flat, slot):
        page = pt_ref[jnp.minimum(flat, total_pages - 1)]
        return (pltpu.make_async_copy(ck_hbm.at[page], kring.at[slot], sems.at[0, slot]),
                pltpu.make_async_copy(cv_hbm.at[page], vring.at[slot], sems.at[1, slot]))

    @pl.when((h == 0) & (i == 0))
    def _():
        g_sc[0] = 0
        dm_sc[...] = jnp.zeros_like(dm_sc)
        dl_sc[...] = jnp.zeros_like(dl_sc)
        dacc_sc[...] = jnp.zeros_like(dacc_sc)
        for t in range(ring):
            for cp in page_copies(t, t):
                cp.start()

    def decode_wait():
        g = g_sc[0]
        for t in range(pp):
            for cp in page_copies(g + t, g % ring + t):
                cp.wait()

    def decode_start():
        g = g_sc[0]
        for t in range(pp):
            for cp in page_copies(g + ring + t, g % ring + t):
                cp.start()
        g_sc[0] = g + pp

    def decode_begin():
        g = g_sc[0]
        gc = jnp.minimum(g, total_pages - pp)
        d = dict(active=g < total_pages,
                 b=gc // n_pages, first=(gc % n_pages) == 0, slot0=g % ring, s=[], pv=None)
        q4 = qs_ref[d["b"]]
        lane = lax.broadcasted_iota(jnp.int32, q4.shape, 1)
        q1, q2 = jnp.where(lane < HEAD_DIM, q4, 0.0), jnp.where(lane >= HEAD_DIM, q4, 0.0)
        d["qrows"] = jnp.concatenate([q1, q2], axis=0).astype(BF16)
        kn = kn_ref[d["b"]]
        d["m0"] = jnp.concatenate([jnp.sum(q1 * kn, axis=1, keepdims=True),
                                   jnp.sum(q2 * kn, axis=1, keepdims=True)], axis=0)
        return d

    def decode_scores(d, t):
        kpage = kring[d["slot0"] + t].astype(BF16)
        d["s"].append(lax.dot_general(d["qrows"], kpage, (((1,), (1,)), ((), ())),
                                      preferred_element_type=F32))

    def decode_softmax(d):
        s, first = jnp.concatenate(d["s"], axis=1), d["first"]
        head_of_col = lax.broadcasted_iota(jnp.int32, s.shape, 1) % nh
        head_of_row = lax.broadcasted_iota(jnp.int32, s.shape, 0) % nh
        s = jnp.where(head_of_col == head_of_row, s, -jnp.inf)
        d["m_old"] = jnp.where(first, d["m0"], dm_sc[:, 0:1])
        d["l_old"] = jnp.where(first, 1.0, dl_sc[:, 0:1])
        d["m_new"] = jnp.maximum(d["m_old"], jnp.max(s, axis=1, keepdims=True))
        d["alpha"] = jnp.exp2(d["m_old"] - d["m_new"])
        p = jnp.exp2(s - d["m_new"])
        d["l_new"] = d["alpha"] * d["l_old"] + jnp.sum(p, axis=1, keepdims=True)
        d["p"] = p.astype(BF16)

    def decode_values(d, t):
        prow = kring.shape[1]
        pv = jnp.dot(d["p"][:, t * prow:(t + 1) * prow], vring[d["slot0"] + t].astype(BF16),
                     preferred_element_type=F32)
        d["pv"] = pv if d["pv"] is None else d["pv"] + pv

    def decode_finish(d):
        active, first, b = d["active"], d["first"], d["b"]
        vn = vn_ref[b]
        acc_old = jnp.where(first, jnp.concatenate([vn, vn], axis=0), dacc_sc[...])
        acc_new = d["alpha"] * acc_old + d["pv"]
        m_keep = jnp.where(active, d["m_new"], d["m_old"])
        l_keep = jnp.where(active, d["l_new"], d["l_old"])
        acc_keep = jnp.where(active, acc_new, acc_old)
        dm_sc[...] = jnp.broadcast_to(m_keep, dm_sc.shape)
        dl_sc[...] = jnp.broadcast_to(l_keep, dl_sc.shape)
        dacc_sc[...] = acc_keep
        o = acc_keep / l_keep
        os_ref[b] = o[0:nh] - lam * o[nh:2 * nh]

    qt = q_ref[0].astype(F32).T
    rows = lax.broadcasted_iota(jnp.int32, qt.shape, 0)
    qb = jnp.concatenate([jnp.where(rows < HEAD_DIM, qt, 0.0), jnp.where(rows >= HEAD_DIM, qt, 0.0)],
                         axis=1).astype(BF16)

    rb = tq
    n_rb = tk // rb

    def score_block(c, r):
        start = pl.multiple_of(c * tk + r * rb, rb)
        return jnp.dot(k_ref[0, pl.ds(start, rb), :], qb, preferred_element_type=F32)

    def causal_block(r):
        kpos = n_unmasked * tk + r * rb + lax.broadcasted_iota(jnp.int32, (rb, 2 * tq), 0)
        qpos = i * tq + lax.broadcasted_iota(jnp.int32, (rb, 2 * tq), 1) % tq
        return kpos <= qpos

    def region(cur_ref, c_cur, causal, nxt_ref, c_nxt):
        def cur_block(r):
            s = cur_ref[r * rb:(r + 1) * rb, :]
            return jnp.where(causal_block(r), s, -jnp.inf) if causal else s

        assert n_rb == pp
        decode_wait()
        d = decode_begin()
        m_old = m_sc[...]
        m_new = m_old
        for r in range(n_rb):
            decode_scores(d, r)
            m_new = jnp.maximum(m_new, jnp.max(cur_block(r), axis=0, keepdims=True))
        alpha = jnp.exp2(m_old - m_new)
        l_new = alpha * l_sc[...]
        pv = None
        for r in range(n_rb):
            if nxt_ref is not None:
                nxt_ref[r * rb:(r + 1) * rb, :] = score_block(c_nxt, r)
            if r == 0:
                decode_softmax(d)
            p = jnp.exp2(cur_block(r) - m_new)
            l_new = l_new + jnp.sum(p, axis=0, keepdims=True)
            vtt = vt_ref[0, :, pl.ds(pl.multiple_of(c_cur * tk + r * rb, rb), rb)]
            pv_r = jnp.dot(vtt, p.astype(BF16), preferred_element_type=F32)
            pv = pv_r if pv is None else pv + pv_r
            decode_values(d, r)
        l_sc[...] = l_new
        acc_sc[...] = alpha * acc_sc[...] + pv
        m_sc[...] = m_new
        decode_finish(d)
        decode_start()

    krow = lax.broadcasted_iota(jnp.int32, (LANES, 2 * tq), 0)
    s_meta = jnp.where(krow < N_META, jnp.dot(kmeta_ref[0], qb, preferred_element_type=F32), -jnp.inf)
    m_meta = jnp.max(s_meta, axis=0, keepdims=True)
    p_meta = jnp.exp2(s_meta - m_meta)
    m_sc[...] = m_meta
    l_sc[...] = jnp.sum(p_meta, axis=0, keepdims=True)
    acc_sc[...] = jnp.dot(vtmeta_ref[0], p_meta.astype(BF16), preferred_element_type=F32)

    for r in range(n_rb):
        sa_sc[r * rb:(r + 1) * rb, :] = score_block(0, r)

    def pair(t, carry):
        c = 2 * t
        region(sa_sc, c, False, sb_sc, c + 1)
        region(sb_sc, c + 1, False, sa_sc, c + 2)
        return carry

    lax.fori_loop(0, n_unmasked // 2, pair, 0)

    @pl.when(n_unmasked % 2 == 1)
    def _():
        region(sa_sc, n_unmasked - 1, False, sb_sc, n_unmasked)
        region(sb_sc, n_unmasked, True, None, None)

    @pl.when(n_unmasked % 2 == 0)
    def _():
        region(sa_sc, n_unmasked, True, None, None)

    o = acc_sc[...] * (1.0 / l_sc[...])
    o_ref[...] = (o[:, :tq] - lam * o[:, tq:]).T

    @pl.when((h == pl.num_programs(0) - 1) & (i == pl.num_programs(1) - 1))
    def _():
        g = g_sc[0]
        for t in range(ring):
            for cp in page_copies(g + t, (g + t) % ring):
                cp.wait()


def _attention(page_table, lq1, lk1, lq2, lk2, q, katt, vt, kmatt, vtm, qs, kn, vn, cache_k, cache_v, *, tq, tk):
    nh, s, _ = q.shape
    db, n_pages = page_table.shape
    _, prow, width = cache_k.shape
    total_pages = db * n_pages
    n_slices = nh * sum(_n_chunks(i, tq, tk) for i in range(s // tq))
    assert n_slices * DECODE_PAGES_PER_SLICE >= total_pages, "prompt too short to carry the decode stream"
    assert n_pages % DECODE_PAGES_PER_SLICE == 0 and DECODE_RING % DECODE_PAGES_PER_SLICE == 0
    small = lambda a: pl.BlockSpec(a.shape, lambda h, i, pt: (0,) * a.ndim)
    hbm = pl.BlockSpec(memory_space=pl.ANY)
    grid_spec = pltpu.PrefetchScalarGridSpec(
        num_scalar_prefetch=1,
        grid=(nh, s // tq),
        in_specs=[small(lq1), small(lk1), small(lq2), small(lk2),
                  pl.BlockSpec((1, tq, HEAD_V), lambda h, i, pt: (h, i, 0)),
                  pl.BlockSpec((1, s, HEAD_V), lambda h, i, pt: (h, 0, 0)),
                  pl.BlockSpec((1, HEAD_V, s), lambda h, i, pt: (h, 0, 0)),
                  pl.BlockSpec((1, LANES, HEAD_V), lambda h, i, pt: (h, 0, 0)),
                  pl.BlockSpec((1, HEAD_V, LANES), lambda h, i, pt: (h, 0, 0)),
                  small(qs), small(kn), small(vn), hbm, hbm],
        out_specs=[pl.BlockSpec((tq, HEAD_V), lambda h, i, pt: (i, h)),
                   pl.BlockSpec((db, N_ATT_HEADS, width), lambda h, i, pt: (0, 0, 0))],
        scratch_shapes=[pltpu.VMEM((1, 2 * tq), F32), pltpu.VMEM((1, 2 * tq), F32),
                        pltpu.VMEM((HEAD_V, 2 * tq), F32),
                        pltpu.VMEM((tk, 2 * tq), F32), pltpu.VMEM((tk, 2 * tq), F32),
                        pltpu.VMEM((DECODE_RING, prow, width), F32), pltpu.VMEM((DECODE_RING, prow, width), F32),
                        pltpu.SemaphoreType.DMA((2, DECODE_RING)), pltpu.SMEM((1,), jnp.int32),
                        pltpu.VMEM((2 * N_ATT_HEADS, LANES), F32), pltpu.VMEM((2 * N_ATT_HEADS, LANES), F32),
                        pltpu.VMEM((2 * N_ATT_HEADS, width), F32)],
    )
    return pl.pallas_call(
        functools.partial(_attn_kernel, tq=tq, tk=tk, n_pages=n_pages, total_pages=total_pages),
        grid_spec=grid_spec,
        out_shape=(jax.ShapeDtypeStruct((s, D_ATT), F32), jax.ShapeDtypeStruct((db, N_ATT_HEADS, width), F32)),
        compiler_params=pltpu.CompilerParams(dimension_semantics=("arbitrary", "arbitrary"),
                                             vmem_limit_bytes=VMEM_LIMIT),
        name="attention",
    )(page_table.reshape(-1), lq1, lk1, lq2, lk2, q, katt, vt, kmatt, vtm, qs, kn, vn, cache_k, cache_v)


def _epilogue_kernel(x_ref, o_ref, gatt_ref, ylg_ref, sub_ref, wout_ref, npost_ref, y_ref):
    parts = []
    for h in range(N_ATT_HEADS):
        sl = slice(h * HEAD_V, (h + 1) * HEAD_V)
        o = _rms(o_ref[:, sl], sub_ref[...]) * (1.0 - LAM_INIT)
        parts.append((o * _silu(gatt_ref[:, sl])).astype(BF16))
    parts.append(ylg_ref[...].astype(BF16))
    mix = jnp.dot(jnp.concatenate(parts, axis=1), wout_ref[...], preferred_element_type=F32)
    y_ref[...] = x_ref[...] + _rms(mix, npost_ref[...])


def _epilogue(x, o_att, gatt, ylg, sub, wout, npost, *, tr):
    s, d = x.shape
    full = lambda a: pl.BlockSpec(a.shape, lambda i: (0,) * a.ndim)
    rows = lambda w: pl.BlockSpec((tr, w), lambda i: (i, 0))
    return pl.pallas_call(
        _epilogue_kernel,
        grid=(s // tr,),
        in_specs=[rows(d), rows(D_ATT), rows(D_ATT), rows(D_LRU), full(sub), full(wout), full(npost)],
        out_specs=rows(d),
        out_shape=jax.ShapeDtypeStruct((s, d), F32),
        compiler_params=pltpu.CompilerParams(dimension_semantics=("arbitrary",), vmem_limit_bytes=VMEM_LIMIT),
        name="epilogue",
    )(x, o_att, gatt, ylg, sub, wout, npost)


def _sample_pre_kernel(x_ref, npre_ref, win_ref, inv_ref, convw_ref, convb_ref, wg_ref, br_ref, bi_ref,
                       lam_ref, h0_ref, sc_ref,
                       q_ref, k_ref, v_ref, gatt_ref, ylg_ref, h_ref, cnew_ref, *, past_len):
    rows = x_ref.shape[0]
    u = _rms(x_ref[...], npre_ref[...]).astype(BF16)
    z = jnp.dot(u, win_ref[...], preferred_element_type=F32)
    ang = float(past_len) * inv_ref[...]
    cos, sin = jnp.cos(ang), jnp.sin(ang)
    fh = _first_half_mask((rows, LANES))
    sin_signed = jnp.where(fh[0:1], -sin, sin)
    for h in range(N_ATT_HEADS):
        sl = slice(h * HEAD_V, (h + 1) * HEAD_V)
        q_ref[:, sl] = _rope_block(z[:, sl], cos, sin_signed, fh) * Q_SCALE
        k_ref[:, sl] = _rope_block(z[:, D_ATT + h * HEAD_V:D_ATT + (h + 1) * HEAD_V], cos, sin_signed, fh)
    v_ref[...] = z[:, 2 * D_ATT:3 * D_ATT]
    gatt_ref[...] = z[:, 3 * D_ATT:4 * D_ATT]
    xb = z[:, 4 * D_ATT:4 * D_ATT + D_LRU]
    glru = z[:, 4 * D_ATT + D_LRU:]
    xc = convb_ref[...] + convw_ref[CONV_WIDTH - 1:CONV_WIDTH, :] * xb
    for j in range(CONV_WIDTH - 1):
        xc = xc + convw_ref[j:j + 1, :] * sc_ref[:, j * D_LRU:(j + 1) * D_LRU]
    a, uu = _lru_gates(xc, wg_ref, br_ref, bi_ref, lam_ref)
    hn = a * h0_ref[...] + uu
    h_ref[...] = hn
    ylg_ref[...] = hn * _silu(glru)
    cnew_ref[:, 0:(CONV_WIDTH - 2) * D_LRU] = sc_ref[:, D_LRU:]
    cnew_ref[:, (CONV_WIDTH - 2) * D_LRU:] = xb


def _sample_pre(x, npre, win, inv, convw, convb, wg, br, bi, lam, h0, sc, *, past_len):
    rows = x.shape[0]
    out_shape = tuple(jax.ShapeDtypeStruct((rows, w), F32)
                      for w in (D_ATT, D_ATT, D_ATT, D_ATT, D_LRU, D_LRU, (CONV_WIDTH - 1) * D_LRU))
    return pl.pallas_call(
        functools.partial(_sample_pre_kernel, past_len=past_len),
        out_shape=out_shape,
        compiler_params=pltpu.CompilerParams(vmem_limit_bytes=VMEM_LIMIT),
        name="sample_pre",
    )(x, npre, win, inv, convw, convb, wg, br, bi, lam, h0, sc)


def _tile(n, target):
    t = min(n, target)
    assert n % t == 0, (n, t)
    return t


def kernel(x_prompt, x_sample, cache_k, cache_v, state_h, state_conv, page_table, meta_tokens, norm_pre, w_in, conv_w, conv_b, w_gate_r, b_gate_r, w_gate_i, b_gate_i, lru_lambda, lambda_q1, lambda_k1, lambda_q2, lambda_k2, attn_subnorm, w_out, norm_post):
    assert x_prompt.shape[0] == 1 and x_sample.shape[1] == 1 and w_in.shape[0] == 1
    seq, d_model = x_prompt.shape[1], x_prompt.shape[2]
    db = x_sample.shape[0]
    n_pool, page = cache_k.shape[1], cache_k.shape[2]
    past_len = page_table.shape[1] * page

    win = w_in[0].astype(BF16)
    wout = w_out[0].astype(BF16)
    row = lambda a: a.reshape(1, -1)
    npre, npost, sub = row(norm_pre[0]), row(norm_post[0]), row(attn_subnorm[0])
    convw, convb = conv_w[0], row(conv_b[0])
    br, bi, lam = row(b_gate_r[0]), row(b_gate_i[0]), row(lru_lambda[0])
    blocks = jnp.arange(D_LRU) // (D_LRU // N_LRU_BLOCKS)
    same_block = blocks[:, None] == blocks[None, :]
    bdiag = lambda w: jnp.where(same_block, jnp.tile(w.reshape(D_LRU, -1), (1, N_LRU_BLOCKS)), 0.0)
    wg = jnp.concatenate([bdiag(w_gate_r[0]), bdiag(w_gate_i[0])], axis=1).astype(BF16)
    lq1, lk1, lq2, lk2 = row(lambda_q1[0]), row(lambda_k1[0]), row(lambda_q2[0]), row(lambda_k2[0])
    inv = ROPE_THETA ** (-jnp.arange(0, HEAD_DIM, 2, dtype=F32) / HEAD_DIM)
    inv = jnp.tile(inv, LANES // (HEAD_DIM // 2)).reshape(1, LANES)

    xp, xs = x_prompt[0], x_sample[:, 0]
    tr = _tile(seq, 512)
    (q, katt, vt, k_x, v_x, gatt, ylg, kmatt, vtm, kmeta, vmeta, hlast, convlast) = _prompt_pre(
        xp, meta_tokens, npre, win, inv, convw, convb, wg, br, bi, lam, tr=tr)
    qs, ks, vs, gatt_s, ylg_s, h_s, conv_s = _sample_pre(
        xs, npre, win, inv, convw, convb, wg, br, bi, lam, state_h[0],
        state_conv[0].reshape(db, (CONV_WIDTH - 1) * D_LRU), past_len=past_len)

    per_head = lambda a: a.reshape(db, N_ATT_HEADS, HEAD_V)
    o_att, o_s = _attention(page_table, lq1, lk1, lq2, lk2, q, katt, vt, kmatt, vtm,
                            per_head(qs), per_head(ks), per_head(vs),
                            cache_k[0].reshape(n_pool, page * N_ATT_HEADS, HEAD_V),
                            cache_v[0].reshape(n_pool, page * N_ATT_HEADS, HEAD_V),
                            tq=_tile(seq, 256), tk=_tile(seq, 1024))

    y_prompt = _epilogue(xp, o_att, gatt, ylg, sub, wout, npost, tr=tr)
    y_sample = _epilogue(xs, o_s.reshape(db, D_ATT), gatt_s, ylg_s, sub, wout, npost, tr=db)

    t_all = seq + N_META
    k_prompt = jnp.concatenate([kmeta[:N_META], k_x], axis=0).reshape(1, 1, t_all, N_ATT_HEADS, HEAD_V)
    v_prompt = jnp.concatenate([vmeta[:N_META], v_x], axis=0).reshape(1, 1, t_all, N_ATT_HEADS, HEAD_V)
    return (y_prompt[None], y_sample[:, None], k_prompt, v_prompt,
            hlast[0:1][None], convlast[SUBLANES - (CONV_WIDTH - 1):][None, None],
            ks.reshape(1, db, 1, N_ATT_HEADS, HEAD_V), vs.reshape(1, db, 1, N_ATT_HEADS, HEAD_V),
            h_s[None], conv_s.reshape(1, db, CONV_WIDTH - 1, D_LRU))
```

```python
import functools
import math

import jax
import jax.numpy as jnp
from jax import lax
from jax.experimental import pallas as pl
from jax.experimental.pallas import tpu as pltpu

N_META = 16
N_ATT_HEADS = 4
HEAD_DIM = 64
HEAD_V = 2 * HEAD_DIM
D_ATT = N_ATT_HEADS * HEAD_V
D_LRU = 512
N_LRU_BLOCKS = 8
CONV_WIDTH = 4
LRU_C = 8.0
ROPE_THETA = 10000.0
EPS = 1e-6
LAM_INIT = 0.8 - 0.6 * math.exp(-0.3 * 0)
Q_SCALE = HEAD_DIM ** -0.5 * math.log2(math.e)

LANES = 128
SUBLANES = 8
VMEM_LIMIT = 56 * 1024 * 1024

F32 = jnp.float32
BF16 = jnp.bfloat16


def _rms(x, g):
    return x * lax.rsqrt(jnp.mean(x * x, axis=-1, keepdims=True) + EPS) * g


def _silu(x):
    return x * jax.nn.sigmoid(x)


def _first_half_mask(shape):
    lane = lax.broadcasted_iota(jnp.int32, shape, len(shape) - 1)
    return (lane % HEAD_DIM) < (HEAD_DIM // 2)


def _rope_block(zb, cos, sin_signed, first_half):
    half = HEAD_DIM // 2
    partner = jnp.where(first_half, pltpu.roll(zb, LANES - half, axis=1), pltpu.roll(zb, half, axis=1))
    return zb * cos + partner * sin_signed


def _diff_lambda(lq1, lk1, lq2, lk2):
    return (jnp.exp(jnp.sum(lq1[...] * lk1[...])) - jnp.exp(jnp.sum(lq2[...] * lk2[...])) + LAM_INIT)


def _lru_gates(xc, wg_ref, br_ref, bi_ref, lam_ref):
    gates = jnp.dot(xc.astype(BF16), wg_ref[...], preferred_element_type=F32)
    r = jax.nn.sigmoid(gates[:, :D_LRU] + br_ref[...])
    i = jax.nn.sigmoid(gates[:, D_LRU:] + bi_ref[...])
    neg_lam = -lam_ref[...]
    softplus = jnp.maximum(neg_lam, 0.0) + jnp.log1p(jnp.exp(-jnp.abs(neg_lam)))
    log_a = -LRU_C * r * softplus
    a = jnp.exp(log_a)
    u = jnp.sqrt(1.0 - jnp.exp(2.0 * log_a)) * (i * xc)
    return a, u


def _prompt_pre_kernel(x_ref, meta_ref, npre_ref, win_ref, inv_ref, convw_ref, convb_ref, wg_ref,
                       br_ref, bi_ref, lam_ref,
                       q_ref, katt_ref, vt_ref, gatt_ref, ylg_ref,
                       kmatt_ref, vtm_ref, hlast_ref, convlast_ref, k_hbm, v_hbm,
                       cos_tab, sin_tab, conv_carry, h_carry, kbuf, vbuf, kmbuf, vmbuf, sems, msems, *, tr):
    step = pl.program_id(0)
    nh = N_ATT_HEADS

    def row_copies(slot, tile):
        rows = pl.ds((N_META + tile * tr) * nh, tr * nh)
        return (pltpu.make_async_copy(kbuf.at[slot], k_hbm.at[rows], sems.at[0, slot]),
                pltpu.make_async_copy(vbuf.at[slot], v_hbm.at[rows], sems.at[1, slot]))

    def meta_copies():
        rows = pl.ds(0, N_META * nh)
        return (pltpu.make_async_copy(kmbuf, k_hbm.at[rows], msems.at[0]),
                pltpu.make_async_copy(vmbuf, v_hbm.at[rows], msems.at[1]))

    def stage_rows(kdst, vdst, ks, zv, rows):
        for h in range(nh):
            kdst[pl.ds(h, rows, stride=nh), :] = ks[h][0:rows]
            vdst[pl.ds(h, rows, stride=nh), :] = zv[0:rows, h * HEAD_V:(h + 1) * HEAD_V]

    def rope_tables(pos0, rows):
        base = pos0.astype(F32) * inv_ref[...]
        cb, sb = jnp.cos(base), jnp.sin(base)
        cr, sr = cos_tab[0:rows, :], sin_tab[0:rows, :]
        cos = cr * cb - sr * sb
        sin = sr * cb + cr * sb
        return cos, jnp.where(_first_half_mask((rows, LANES)), -sin, sin)

    def project(x, pos0, rows):
        u = _rms(x, npre_ref[...]).astype(BF16)
        cos, sin_signed = rope_tables(pos0, rows)
        fh = _first_half_mask((rows, LANES))
        zq = jnp.dot(u, win_ref[:, 0:D_ATT], preferred_element_type=F32)
        zk = jnp.dot(u, win_ref[:, D_ATT:2 * D_ATT], preferred_element_type=F32)
        zv = jnp.dot(u, win_ref[:, 2 * D_ATT:3 * D_ATT], preferred_element_type=F32)
        qs, ks = [], []
        for h in range(N_ATT_HEADS):
            sl = slice(h * HEAD_V, (h + 1) * HEAD_V)
            qs.append(_rope_block(zq[:, sl], cos, sin_signed, fh) * Q_SCALE)
            ks.append(_rope_block(zk[:, sl], cos, sin_signed, fh))
        return u, qs, ks, zv

    def lru(xb, glru, rows, store_y):
        xp = jnp.concatenate([conv_carry[...], xb], axis=0)
        off = SUBLANES - (CONV_WIDTH - 1)
        xc = convb_ref[...] + convw_ref[0:1, :] * xp[off:off + rows]
        for j in range(1, CONV_WIDTH):
            xc = xc + convw_ref[j:j + 1, :] * xp[off + j:off + j + rows]
        conv_carry[...] = xp[rows:rows + SUBLANES]
        a, u = _lru_gates(xc, wg_ref, br_ref, bi_ref, lam_ref)
        row = lax.broadcasted_iota(jnp.int32, (rows, D_LRU), 0) % SUBLANES
        d = 1
        while d < SUBLANES:
            valid = row >= d
            u = jnp.where(valid, a * pltpu.roll(u, d, axis=0) + u, u)
            a = jnp.where(valid, a * pltpu.roll(a, d, axis=0), a)
            d *= 2
        h = h_carry[...]
        for g in range(rows // SUBLANES):
            sl = slice(g * SUBLANES, (g + 1) * SUBLANES)
            hg = a[sl] * h + u[sl]
            if store_y:
                ylg_ref[sl, :] = hg * _silu(glru[sl])
            h = jnp.broadcast_to(hg[SUBLANES - 1:SUBLANES, :], (SUBLANES, D_LRU))
        h_carry[...] = h

    @pl.when(step == 0)
    def _():
        r = lax.broadcasted_iota(jnp.int32, (tr, LANES), 0).astype(F32)
        ang = r * inv_ref[...]
        cos_tab[...] = jnp.cos(ang)
        sin_tab[...] = jnp.sin(ang)
        conv_carry[...] = jnp.zeros_like(conv_carry)
        h_carry[...] = jnp.zeros_like(h_carry)
        xm = jnp.concatenate([meta_ref[...], jnp.zeros((LANES - N_META, meta_ref.shape[1]), F32)], axis=0)
        u, _, ks, zv = project(xm, jnp.zeros((), jnp.int32), LANES)
        for h in range(N_ATT_HEADS):
            sl = slice(h * HEAD_V, (h + 1) * HEAD_V)
            kmatt_ref[h] = ks[h].astype(BF16)
            vtm_ref[h] = zv[:, sl].T.astype(BF16)
        stage_rows(kmbuf, vmbuf, ks, zv, N_META)
        for cp in meta_copies():
            cp.start()
        um = u[0:N_META]
        xb = jnp.dot(um, win_ref[:, 4 * D_ATT:4 * D_ATT + D_LRU], preferred_element_type=F32)
        lru(xb, None, N_META, False)

    u, qs, ks, zv = project(x_ref[...], N_META + step * tr, tr)
    for h in range(N_ATT_HEADS):
        sl = slice(h * HEAD_V, (h + 1) * HEAD_V)
        q_ref[h] = qs[h].astype(BF16)
        katt_ref[h] = ks[h].astype(BF16)
        vt_ref[h] = zv[:, sl].T.astype(BF16)
    slot = step % 2

    @pl.when(step >= 2)
    def _():
        for cp in row_copies(slot, step - 2):
            cp.wait()

    stage_rows(kbuf.at[slot], vbuf.at[slot], ks, zv, tr)
    for cp in row_copies(slot, step):
        cp.start()
    gatt_ref[...] = jnp.dot(u, win_ref[:, 3 * D_ATT:4 * D_ATT], preferred_element_type=F32)
    xb = jnp.dot(u, win_ref[:, 4 * D_ATT:4 * D_ATT + D_LRU], preferred_element_type=F32)
    glru = jnp.dot(u, win_ref[:, 4 * D_ATT + D_LRU:], preferred_element_type=F32)
    lru(xb, glru, tr, True)
    hlast_ref[...] = h_carry[...]
    convlast_ref[...] = conv_carry[...]

    @pl.when(step == pl.num_programs(0) - 1)
    def _():
        for cp in meta_copies():
            cp.wait()
        for cp in row_copies(slot, step):
            cp.wait()

        @pl.when(step >= 1)
        def _():
            for cp in row_copies(1 - slot, step - 1):
                cp.wait()


def _prompt_pre(x, meta, npre, win, inv, convw, convb, wg, br, bi, lam, *, tr):
    s, d = x.shape
    nt = s // tr
    full = lambda a: pl.BlockSpec(a.shape, lambda i: (0,) * a.ndim)
    out_shape = (
        jax.ShapeDtypeStruct((N_ATT_HEADS, s, HEAD_V), BF16),
        jax.ShapeDtypeStruct((N_ATT_HEADS, s, HEAD_V), BF16),
        jax.ShapeDtypeStruct((N_ATT_HEADS, HEAD_V, s), BF16),
        jax.ShapeDtypeStruct((s, D_ATT), F32),
        jax.ShapeDtypeStruct((s, D_LRU), F32),
        jax.ShapeDtypeStruct((N_ATT_HEADS, LANES, HEAD_V), BF16),
        jax.ShapeDtypeStruct((N_ATT_HEADS, HEAD_V, LANES), BF16),
        jax.ShapeDtypeStruct((SUBLANES, D_LRU), F32),
        jax.ShapeDtypeStruct((SUBLANES, D_LRU), F32),
        jax.ShapeDtypeStruct(((s + N_META) * N_ATT_HEADS, HEAD_V), F32),
        jax.ShapeDtypeStruct(((s + N_META) * N_ATT_HEADS, HEAD_V), F32),
    )
    out_specs = (
        pl.BlockSpec((N_ATT_HEADS, tr, HEAD_V), lambda i: (0, i, 0)),
        pl.BlockSpec((N_ATT_HEADS, tr, HEAD_V), lambda i: (0, i, 0)),
        pl.BlockSpec((N_ATT_HEADS, HEAD_V, tr), lambda i: (0, 0, i)),
        pl.BlockSpec((tr, D_ATT), lambda i: (i, 0)),
        pl.BlockSpec((tr, D_LRU), lambda i: (i, 0)),
    ) + tuple(pl.BlockSpec(o.shape, lambda i, n=len(o.shape): (0,) * n) for o in out_shape[5:9]) + (
        pl.BlockSpec(memory_space=pl.ANY), pl.BlockSpec(memory_space=pl.ANY))
    return pl.pallas_call(
        functools.partial(_prompt_pre_kernel, tr=tr),
        grid=(nt,),
        in_specs=[pl.BlockSpec((tr, d), lambda i: (i, 0))] + [full(a) for a in (meta, npre, win, inv, convw, convb, wg, br, bi, lam)],
        out_specs=out_specs,
        out_shape=out_shape,
        scratch_shapes=[pltpu.VMEM((tr, LANES), F32), pltpu.VMEM((tr, LANES), F32),
                        pltpu.VMEM((SUBLANES, D_LRU), F32), pltpu.VMEM((SUBLANES, D_LRU), F32),
                        pltpu.VMEM((2, tr * N_ATT_HEADS, HEAD_V), F32), pltpu.VMEM((2, tr * N_ATT_HEADS, HEAD_V), F32),
                        pltpu.VMEM((N_META * N_ATT_HEADS, HEAD_V), F32), pltpu.VMEM((N_META * N_ATT_HEADS, HEAD_V), F32),
                        pltpu.SemaphoreType.DMA((2, 2)), pltpu.SemaphoreType.DMA((2,))],
        compiler_params=pltpu.CompilerParams(dimension_semantics=("arbitrary",), vmem_limit_bytes=VMEM_LIMIT),
        name="prompt_pre",
    )(x, meta, npre, win, inv, convw, convb, wg, br, bi, lam)


DECODE_PAGES_PER_REGION = 8
DECODE_RING = 32


def _n_chunks(i, tq, tk):
    return (i * tq) // tk + 1


def _attn_kernel(pt_ref, lq1, lk1, lq2, lk2, q_ref, k_ref, vt_ref, kmeta_ref, vtmeta_ref,
                 qs_ref, kn_ref, vn_ref, ck_hbm, cv_hbm, o_ref, os_ref,
                 m_sc, l_sc, acc_sc, sa_sc, sb_sc, kring, vring, sems, g_sc, dm_sc, dl_sc, dacc_sc,
                 *, tq, tk, n_pages, total_pages):
    h, i = pl.program_id(0), pl.program_id(1)
    pp, ring, nh = DECODE_PAGES_PER_REGION, DECODE_RING, N_ATT_HEADS
    n_unmasked = _n_chunks(i, tq, tk) - 1
    lam = _diff_lambda(lq1, lk1, lq2, lk2)

    def page_copies(flat, slot):
        page = pt_ref[jnp.minimum(flat, total_pages - 1)]
        return (pltpu.make_async_copy(ck_hbm.at[page], kring.at[slot], sems.at[0, slot]),
                pltpu.make_async_copy(cv_hbm.at[page], vring.at[slot], sems.at[1, slot]))

    @pl.when((h == 0) & (i == 0))
    def _():
        g_sc[0] = 0
        dm_sc[...] = jnp.zeros_like(dm_sc)
        dl_sc[...] = jnp.zeros_like(dl_sc)
        dacc_sc[...] = jnp.zeros_like(dacc_sc)
        for t in range(ring):
            for cp in page_copies(t, t):
                cp.start()

    def decode_wait(g, n):
        for t in range(n):
            for cp in page_copies(g + t, (g + t) % ring):
                cp.wait()

    def decode_start(g, n):
        for t in range(n):
            for cp in page_copies(g + ring + t, (g + t) % ring):
                cp.start()
        g_sc[0] = g + n

    def decode_begin(g):
        gc = jnp.minimum(g, total_pages - pp)
        d = dict(active=g < total_pages,
                 b=gc // n_pages, first=(gc % n_pages) == 0, slot0=g % ring, s=[], pv=None)
        q4 = qs_ref[d["b"]]
        lane = lax.broadcasted_iota(jnp.int32, q4.shape, 1)
        q1, q2 = jnp.where(lane < HEAD_DIM, q4, 0.0), jnp.where(lane >= HEAD_DIM, q4, 0.0)
        d["qrows"] = jnp.concatenate([q1, q2], axis=0).astype(BF16)
        kn = kn_ref[d["b"]]
        d["m0"] = jnp.concatenate([jnp.sum(q1 * kn, axis=1, keepdims=True),
                                   jnp.sum(q2 * kn, axis=1, keepdims=True)], axis=0)
        return d

    def decode_scores(d, t):
        kpage = kring[d["slot0"] + t].astype(BF16)
        d["s"].append(lax.dot_general(d["qrows"], kpage, (((1,), (1,)), ((), ())),
                                      preferred_element_type=F32))

    def decode_softmax(d):
        s, first = jnp.concatenate(d["s"], axis=1), d["first"]
        head_of_col = lax.broadcasted_iota(jnp.int32, s.shape, 1) % nh
        head_of_row = lax.broadcasted_iota(jnp.int32, s.shape, 0) % nh
        s = jnp.where(head_of_col == head_of_row, s, -jnp.inf)
        d["m_old"] = jnp.where(first, d["m0"], dm_sc[:, 0:1])
        d["l_old"] = jnp.where(first, 1.0, dl_sc[:, 0:1])
        d["m_new"] = jnp.maximum(d["m_old"], jnp.max(s, axis=1, keepdims=True))
        d["alpha"] = jnp.exp2(d["m_old"] - d["m_new"])
        p = jnp.exp2(s - d["m_new"])
        d["l_new"] = d["alpha"] * d["l_old"] + jnp.sum(p, axis=1, keepdims=True)
        d["p"] = p.astype(BF16)

    def decode_values(d, t):
        prow = kring.shape[1]
        pv = jnp.dot(d["p"][:, t * prow:(t + 1) * prow], vring[d["slot0"] + t].astype(BF16),
                     preferred_element_type=F32)
        d["pv"] = pv if d["pv"] is None else d["pv"] + pv

    def decode_finish(d):
        active, first, b = d["active"], d["first"], d["b"]
        vn = vn_ref[b]
        acc_old = jnp.where(first, jnp.concatenate([vn, vn], axis=0), dacc_sc[...])
        acc_new = d["alpha"] * acc_old + d["pv"]
        m_keep = jnp.where(active, d["m_new"], d["m_old"])
        l_keep = jnp.where(active, d["l_new"], d["l_old"])
        acc_keep = jnp.where(active, acc_new, acc_old)
        dm_sc[...] = jnp.broadcast_to(m_keep, dm_sc.shape)
        dl_sc[...] = jnp.broadcast_to(l_keep, dl_sc.shape)
        dacc_sc[...] = acc_keep
        o = acc_keep / l_keep
        os_ref[b] = o[0:nh] - lam * o[nh:2 * nh]

    qt = q_ref[0].astype(F32).T
    rows = lax.broadcasted_iota(jnp.int32, qt.shape, 0)
    qb = jnp.concatenate([jnp.where(rows < HEAD_DIM, qt, 0.0), jnp.where(rows >= HEAD_DIM, qt, 0.0)],
                         axis=1).astype(BF16)

    rb = min(tq, 256)
    n_rb = tk // rb
    assert pp % n_rb == 0
    ppb = pp // n_rb

    def score_block(c, r):
        start = pl.multiple_of(c * tk + r * rb, rb)
        return jnp.dot(k_ref[0, pl.ds(start, rb), :], qb, preferred_element_type=F32)

    def causal_block(r):
        kpos = n_unmasked * tk + r * rb + lax.broadcasted_iota(jnp.int32, (rb, 2 * tq), 0)
        qpos = i * tq + lax.broadcasted_iota(jnp.int32, (rb, 2 * tq), 1) % tq
        return kpos <= qpos

    def region(cur_ref, c_cur, causal, nxt_ref, c_nxt, g):
        def cur_block(r):
            s = cur_ref[r * rb:(r + 1) * rb, :]
            return jnp.where(causal_block(r), s, -jnp.inf) if causal else s

        d = decode_begin(g)
        m_old = m_sc[...]
        m_new = m_old
        for r in range(n_rb):
            for u in range(ppb):
                decode_scores(d, r * ppb + u)
            m_new = jnp.maximum(m_new, jnp.max(cur_block(r), axis=0, keepdims=True))
        alpha = jnp.exp2(m_old - m_new)
        l_new = alpha * l_sc[...]
        pv = None
        for r in range(n_rb):
            if nxt_ref is not None:
                nxt_ref[r * rb:(r + 1) * rb, :] = score_block(c_nxt, r)
            if r == 0:
                decode_softmax(d)
            p = jnp.exp2(cur_block(r) - m_new)
            l_new = l_new + jnp.sum(p, axis=0, keepdims=True)
            vtt = vt_ref[0, :, pl.ds(pl.multiple_of(c_cur * tk + r * rb, rb), rb)]
            pv_r = jnp.dot(vtt, p.astype(BF16), preferred_element_type=F32)
            pv = pv_r if pv is None else pv + pv_r
            for u in range(ppb):
                decode_values(d, r * ppb + u)
        l_sc[...] = l_new
        acc_sc[...] = alpha * acc_sc[...] + pv
        m_sc[...] = m_new
        decode_finish(d)

    krow = lax.broadcasted_iota(jnp.int32, (LANES, 2 * tq), 0)
    s_meta = jnp.where(krow < N_META, jnp.dot(kmeta_ref[0], qb, preferred_element_type=F32), -jnp.inf)
    m_meta = jnp.max(s_meta, axis=0, keepdims=True)
    p_meta = jnp.exp2(s_meta - m_meta)
    m_sc[...] = m_meta
    l_sc[...] = jnp.sum(p_meta, axis=0, keepdims=True)
    acc_sc[...] = jnp.dot(vtmeta_ref[0], p_meta.astype(BF16), preferred_element_type=F32)

    for r in range(n_rb):
        sa_sc[r * rb:(r + 1) * rb, :] = score_block(0, r)

    def pair(t, carry):
        c, g = 2 * t, g_sc[0]
        decode_wait(g, 2 * pp)
        region(sa_sc, c, False, sb_sc, c + 1, g)
        region(sb_sc, c + 1, False, sa_sc, c + 2, g + pp)
        decode_start(g, 2 * pp)
        return carry

    lax.fori_loop(0, n_unmasked // 2, pair, 0)

    @pl.when(n_unmasked % 2 == 1)
    def _():
        g = g_sc[0]
        decode_wait(g, 2 * pp)
        region(sa_sc, n_unmasked - 1, False, sb_sc, n_unmasked, g)
        region(sb_sc, n_unmasked, True, None, None, g + pp)
        decode_start(g, 2 * pp)

    @pl.when(n_unmasked % 2 == 0)
    def _():
        g = g_sc[0]
        decode_wait(g, pp)
        region(sa_sc, n_unmasked, True, None, None, g)
        decode_start(g, pp)

    o = acc_sc[...] * (1.0 / l_sc[...])
    o_ref[...] = (o[:, :tq] - lam * o[:, tq:]).T

    @pl.when((h == pl.num_programs(0) - 1) & (i == pl.num_programs(1) - 1))
    def _():
        g = g_sc[0]
        for t in range(ring):
            for cp in page_copies(g + t, (g + t) % ring):
                cp.wait()


def _attention(page_table, lq1, lk1, lq2, lk2, q, katt, vt, kmatt, vtm, qs, kn, vn, cache_k, cache_v, *, tq, tk):
    nh, s, _ = q.shape
    db, n_pages = page_table.shape
    _, prow, width = cache_k.shape
    total_pages = db * n_pages
    n_slices = nh * sum(_n_chunks(i, tq, tk) for i in range(s // tq))
    assert n_slices * DECODE_PAGES_PER_REGION >= total_pages, "prompt too short to carry the decode stream"
    assert n_pages % DECODE_PAGES_PER_REGION == 0 and DECODE_RING >= 4 * DECODE_PAGES_PER_REGION
    small = lambda a: pl.BlockSpec(a.shape, lambda h, i, pt: (0,) * a.ndim)
    hbm = pl.BlockSpec(memory_space=pl.ANY)
    grid_spec = pltpu.PrefetchScalarGridSpec(
        num_scalar_prefetch=1,
        grid=(nh, s // tq),
        in_specs=[small(lq1), small(lk1), small(lq2), small(lk2),
                  pl.BlockSpec((1, tq, HEAD_V), lambda h, i, pt: (h, i, 0)),
                  pl.BlockSpec((1, s, HEAD_V), lambda h, i, pt: (h, 0, 0)),
                  pl.BlockSpec((1, HEAD_V, s), lambda h, i, pt: (h, 0, 0)),
                  pl.BlockSpec((1, LANES, HEAD_V), lambda h, i, pt: (h, 0, 0)),
                  pl.BlockSpec((1, HEAD_V, LANES), lambda h, i, pt: (h, 0, 0)),
                  small(qs), small(kn), small(vn), hbm, hbm],
        out_specs=[pl.BlockSpec((tq, HEAD_V), lambda h, i, pt: (i, h)),
                   pl.BlockSpec((db, N_ATT_HEADS, width), lambda h, i, pt: (0, 0, 0))],
        scratch_shapes=[pltpu.VMEM((1, 2 * tq), F32), pltpu.VMEM((1, 2 * tq), F32),
                        pltpu.VMEM((HEAD_V, 2 * tq), F32),
                        pltpu.VMEM((tk, 2 * tq), F32), pltpu.VMEM((tk, 2 * tq), F32),
                        pltpu.VMEM((DECODE_RING, prow, width), F32), pltpu.VMEM((DECODE_RING, prow, width), F32),
                        pltpu.SemaphoreType.DMA((2, DECODE_RING)), pltpu.SMEM((1,), jnp.int32),
                        pltpu.VMEM((2 * N_ATT_HEADS, LANES), F32), pltpu.VMEM((2 * N_ATT_HEADS, LANES), F32),
                        pltpu.VMEM((2 * N_ATT_HEADS, width), F32)],
    )
    return pl.pallas_call(
        functools.partial(_attn_kernel, tq=tq, tk=tk, n_pages=n_pages, total_pages=total_pages),
        grid_spec=grid_spec,
        out_shape=(jax.ShapeDtypeStruct((s, D_ATT), F32), jax.ShapeDtypeStruct((db, N_ATT_HEADS, width), F32)),
        compiler_params=pltpu.CompilerParams(dimension_semantics=("arbitrary", "arbitrary"),
                                             vmem_limit_bytes=VMEM_LIMIT),
        name="attention",
    )(page_table.reshape(-1), lq1, lk1, lq2, lk2, q, katt, vt, kmatt, vtm, qs, kn, vn, cache_k, cache_v)


def _epilogue_kernel(x_ref, o_ref, gatt_ref, ylg_ref, sub_ref, wout_ref, npost_ref, y_ref):
    parts = []
    for h in range(N_ATT_HEADS):
        sl = slice(h * HEAD_V, (h + 1) * HEAD_V)
        o = _rms(o_ref[:, sl], sub_ref[...]) * (1.0 - LAM_INIT)
        parts.append((o * _silu(gatt_ref[:, sl])).astype(BF16))
    parts.append(ylg_ref[...].astype(BF16))
    mix = jnp.dot(jnp.concatenate(parts, axis=1), wout_ref[...], preferred_element_type=F32)
    y_ref[...] = x_ref[...] + _rms(mix, npost_ref[...])


def _epilogue(x, o_att, gatt, ylg, sub, wout, npost, *, tr):
    s, d = x.shape
    full = lambda a: pl.BlockSpec(a.shape, lambda i: (0,) * a.ndim)
    rows = lambda w: pl.BlockSpec((tr, w), lambda i: (i, 0))
    return pl.pallas_call(
        _epilogue_kernel,
        grid=(s // tr,),
        in_specs=[rows(d), rows(D_ATT), rows(D_ATT), rows(D_LRU), full(sub), full(wout), full(npost)],
        out_specs=rows(d),
        out_shape=jax.ShapeDtypeStruct((s, d), F32),
        compiler_params=pltpu.CompilerParams(dimension_semantics=("arbitrary",), vmem_limit_bytes=VMEM_LIMIT),
        name="epilogue",
    )(x, o_att, gatt, ylg, sub, wout, npost)


def _sample_pre_kernel(x_ref, npre_ref, win_ref, inv_ref, convw_ref, convb_ref, wg_ref, br_ref, bi_ref,
                       lam_ref, h0_ref, sc_ref,
                       q_ref, k_ref, v_ref, gatt_ref, ylg_ref, h_ref, cnew_ref, *, past_len):
    rows = x_ref.shape[0]
    u = _rms(x_ref[...], npre_ref[...]).astype(BF16)
    z = jnp.dot(u, win_ref[...], preferred_element_type=F32)
    ang = float(past_len) * inv_ref[...]
    cos, sin = jnp.cos(ang), jnp.sin(ang)
    fh = _first_half_mask((rows, LANES))
    sin_signed = jnp.where(fh[0:1], -sin, sin)
    for h in range(N_ATT_HEADS):
        sl = slice(h * HEAD_V, (h + 1) * HEAD_V)
        q_ref[:, sl] = _rope_block(z[:, sl], cos, sin_signed, fh) * Q_SCALE
        k_ref[:, sl] = _rope_block(z[:, D_ATT + h * HEAD_V:D_ATT + (h + 1) * HEAD_V], cos, sin_signed, fh)
    v_ref[...] = z[:, 2 * D_ATT:3 * D_ATT]
    gatt_ref[...] = z[:, 3 * D_ATT:4 * D_ATT]
    xb = z[:, 4 * D_ATT:4 * D_ATT + D_LRU]
    glru = z[:, 4 * D_ATT + D_LRU:]
    xc = convb_ref[...] + convw_ref[CONV_WIDTH - 1:CONV_WIDTH, :] * xb
    for j in range(CONV_WIDTH - 1):
        xc = xc + convw_ref[j:j + 1, :] * sc_ref[:, j * D_LRU:(j + 1) * D_LRU]
    a, uu = _lru_gates(xc, wg_ref, br_ref, bi_ref, lam_ref)
    hn = a * h0_ref[...] + uu
    h_ref[...] = hn
    ylg_ref[...] = hn * _silu(glru)
    cnew_ref[:, 0:(CONV_WIDTH - 2) * D_LRU] = sc_ref[:, D_LRU:]
    cnew_ref[:, (CONV_WIDTH - 2) * D_LRU:] = xb


def _sample_pre(x, npre, win, inv, convw, convb, wg, br, bi, lam, h0, sc, *, past_len):
    rows = x.shape[0]
    out_shape = tuple(jax.ShapeDtypeStruct((rows, w), F32)
                      for w in (D_ATT, D_ATT, D_ATT, D_ATT, D_LRU, D_LRU, (CONV_WIDTH - 1) * D_LRU))
    return pl.pallas_call(
        functools.partial(_sample_pre_kernel, past_len=past_len),
        out_shape=out_shape,
        compiler_params=pltpu.CompilerParams(vmem_limit_bytes=VMEM_LIMIT),
        name="sample_pre",
    )(x, npre, win, inv, convw, convb, wg, br, bi, lam, h0, sc)


def _tile(n, target):
    t = min(n, target)
    assert n % t == 0, (n, t)
    return t


def kernel(x_prompt, x_sample, cache_k, cache_v, state_h, state_conv, page_table, meta_tokens, norm_pre, w_in, conv_w, conv_b, w_gate_r, b_gate_r, w_gate_i, b_gate_i, lru_lambda, lambda_q1, lambda_k1, lambda_q2, lambda_k2, attn_subnorm, w_out, norm_post):
    assert x_prompt.shape[0] == 1 and x_sample.shape[1] == 1 and w_in.shape[0] == 1
    seq, d_model = x_prompt.shape[1], x_prompt.shape[2]
    db = x_sample.shape[0]
    n_pool, page = cache_k.shape[1], cache_k.shape[2]
    past_len = page_table.shape[1] * page

    win = w_in[0].astype(BF16)
    wout = w_out[0].astype(BF16)
    row = lambda a: a.reshape(1, -1)
    npre, npost, sub = row(norm_pre[0]), row(norm_post[0]), row(attn_subnorm[0])
    convw, convb = conv_w[0], row(conv_b[0])
    br, bi, lam = row(b_gate_r[0]), row(b_gate_i[0]), row(lru_lambda[0])
    blocks = jnp.arange(D_LRU) // (D_LRU // N_LRU_BLOCKS)
    same_block = blocks[:, None] == blocks[None, :]
    bdiag = lambda w: jnp.where(same_block, jnp.tile(w.reshape(D_LRU, -1), (1, N_LRU_BLOCKS)), 0.0)
    wg = jnp.concatenate([bdiag(w_gate_r[0]), bdiag(w_gate_i[0])], axis=1).astype(BF16)
    lq1, lk1, lq2, lk2 = row(lambda_q1[0]), row(lambda_k1[0]), row(lambda_q2[0]), row(lambda_k2[0])
    inv = ROPE_THETA ** (-jnp.arange(0, HEAD_DIM, 2, dtype=F32) / HEAD_DIM)
    inv = jnp.tile(inv, LANES // (HEAD_DIM // 2)).reshape(1, LANES)

    xp, xs = x_prompt[0], x_sample[:, 0]
    tr = _tile(seq, 512)
    (q, katt, vt, gatt, ylg, kmatt, vtm, hlast, convlast, k_rows, v_rows) = _prompt_pre(
        xp, meta_tokens, npre, win, inv, convw, convb, wg, br, bi, lam, tr=tr)
    qs, ks, vs, gatt_s, ylg_s, h_s, conv_s = _sample_pre(
        xs, npre, win, inv, convw, convb, wg, br, bi, lam, state_h[0],
        state_conv[0].reshape(db, (CONV_WIDTH - 1) * D_LRU), past_len=past_len)

    per_head = lambda a: a.reshape(db, N_ATT_HEADS, HEAD_V)
    o_att, o_s = _attention(page_table, lq1, lk1, lq2, lk2, q, katt, vt, kmatt, vtm,
                            per_head(qs), per_head(ks), per_head(vs),
                            cache_k[0].reshape(n_pool, page * N_ATT_HEADS, HEAD_V),
                            cache_v[0].reshape(n_pool, page * N_ATT_HEADS, HEAD_V),
                            tq=_tile(seq, 512), tk=_tile(seq, 1024))

    y_prompt = _epilogue(xp, o_att, gatt, ylg, sub, wout, npost, tr=tr)
    y_sample = _epilogue(xs, o_s.reshape(db, D_ATT), gatt_s, ylg_s, sub, wout, npost, tr=db)

    t_all = seq + N_META
    k_prompt = k_rows.reshape(1, 1, t_all, N_ATT_HEADS, HEAD_V)
    v_prompt = v_rows.reshape(1, 1, t_all, N_ATT_HEADS, HEAD_V)
    return (y_prompt[None], y_sample[:, None], k_prompt, v_prompt,
            hlast[0:1][None], convlast[SUBLANES - (CONV_WIDTH - 1):][None, None],
            ks.reshape(1, db, 1, N_ATT_HEADS, HEAD_V), vs.reshape(1, db, 1, N_ATT_HEADS, HEAD_V),
            h_s[None], conv_s.reshape(1, db, CONV_WIDTH - 1, D_LRU))
```

```python
import functools
import math

import jax
import jax.numpy as jnp
from jax import lax
from jax.experimental import pallas as pl
from jax.experimental.pallas import tpu as pltpu

N_META = 16
N_ATT_HEADS = 4
HEAD_DIM = 64
HEAD_V = 2 * HEAD_DIM
D_ATT = N_ATT_HEADS * HEAD_V
D_LRU = 512
N_LRU_BLOCKS = 8
CONV_WIDTH = 4
LRU_C = 8.0
ROPE_THETA = 10000.0
EPS = 1e-6
LAM_INIT = 0.8 - 0.6 * math.exp(-0.3 * 0)
Q_SCALE = HEAD_DIM ** -0.5 * math.log2(math.e)

LANES = 128
SUBLANES = 8
VMEM_LIMIT = 56 * 1024 * 1024

F32 = jnp.float32
BF16 = jnp.bfloat16


def _rms(x, g):
    return x * lax.rsqrt(jnp.mean(x * x, axis=-1, keepdims=True) + EPS) * g


def _silu(x):
    return x * jax.nn.sigmoid(x)


def _first_half_mask(shape):
    lane = lax.broadcasted_iota(jnp.int32, shape, len(shape) - 1)
    return (lane % HEAD_DIM) < (HEAD_DIM // 2)


def _rope_block(zb, cos, sin_signed, first_half):
    half = HEAD_DIM // 2
    partner = jnp.where(first_half, pltpu.roll(zb, LANES - half, axis=1), pltpu.roll(zb, half, axis=1))
    return zb * cos + partner * sin_signed


def _diff_lambda(lq1, lk1, lq2, lk2):
    return (jnp.exp(jnp.sum(lq1[...] * lk1[...])) - jnp.exp(jnp.sum(lq2[...] * lk2[...])) + LAM_INIT)


def _lru_gates(xc, wg_ref, br_ref, bi_ref, lam_ref):
    gates = jnp.dot(xc.astype(BF16), wg_ref[...], preferred_element_type=F32)
    r = jax.nn.sigmoid(gates[:, :D_LRU] + br_ref[...])
    i = jax.nn.sigmoid(gates[:, D_LRU:] + bi_ref[...])
    neg_lam = -lam_ref[...]
    softplus = jnp.maximum(neg_lam, 0.0) + jnp.log1p(jnp.exp(-jnp.abs(neg_lam)))
    log_a = -LRU_C * r * softplus
    a = jnp.exp(log_a)
    one_minus_a2 = 1.0 - jnp.exp(2.0 * log_a)
    root = jnp.where(one_minus_a2 > 0.0, one_minus_a2 * lax.rsqrt(one_minus_a2), 0.0)
    u = root * (i * xc)
    return a, u


def _prompt_pre_kernel(x_ref, meta_ref, npre_ref, win_ref, inv_ref, convw_ref, convb_ref, wg_ref,
                       br_ref, bi_ref, lam_ref,
                       q_ref, katt_ref, vt_ref, gatt_ref, ylg_ref,
                       kmatt_ref, vtm_ref, hlast_ref, convlast_ref, k_hbm, v_hbm,
                       cos_tab, sin_tab, conv_carry, h_carry, kbuf, vbuf, kmbuf, vmbuf, sems, msems, *, tr):
    step = pl.program_id(0)
    nh = N_ATT_HEADS

    def row_copies(slot, tile):
        rows = pl.ds((N_META + tile * tr) * nh, tr * nh)
        return (pltpu.make_async_copy(kbuf.at[slot], k_hbm.at[rows], sems.at[0, slot]),
                pltpu.make_async_copy(vbuf.at[slot], v_hbm.at[rows], sems.at[1, slot]))

    def meta_copies():
        rows = pl.ds(0, N_META * nh)
        return (pltpu.make_async_copy(kmbuf, k_hbm.at[rows], msems.at[0]),
                pltpu.make_async_copy(vmbuf, v_hbm.at[rows], msems.at[1]))

    def stage_rows(kdst, vdst, ks, zv, rows):
        for h in range(nh):
            kdst[pl.ds(h, rows, stride=nh), :] = ks[h][0:rows]
            vdst[pl.ds(h, rows, stride=nh), :] = zv[0:rows, h * HEAD_V:(h + 1) * HEAD_V]

    def rope_tables(pos0, rows):
        base = pos0.astype(F32) * inv_ref[...]
        cb, sb = jnp.cos(base), jnp.sin(base)
        cr, sr = cos_tab[0:rows, :], sin_tab[0:rows, :]
        cos = cr * cb - sr * sb
        sin = sr * cb + cr * sb
        return cos, jnp.where(_first_half_mask((rows, LANES)), -sin, sin)

    def project(x, pos0, rows):
        u = _rms(x, npre_ref[...]).astype(BF16)
        cos, sin_signed = rope_tables(pos0, rows)
        fh = _first_half_mask((rows, LANES))
        zq = jnp.dot(u, win_ref[:, 0:D_ATT], preferred_element_type=F32)
        zk = jnp.dot(u, win_ref[:, D_ATT:2 * D_ATT], preferred_element_type=F32)
        zv = jnp.dot(u, win_ref[:, 2 * D_ATT:3 * D_ATT], preferred_element_type=F32)
        qs, ks = [], []
        for h in range(N_ATT_HEADS):
            sl = slice(h * HEAD_V, (h + 1) * HEAD_V)
            qs.append(_rope_block(zq[:, sl], cos, sin_signed, fh) * Q_SCALE)
            ks.append(_rope_block(zk[:, sl], cos, sin_signed, fh))
        return u, qs, ks, zv

    def lru(xb, glru, rows, store_y):
        xp = jnp.concatenate([conv_carry[...], xb], axis=0)
        off = SUBLANES - (CONV_WIDTH - 1)
        xc = convb_ref[...] + convw_ref[0:1, :] * xp[off:off + rows]
        for j in range(1, CONV_WIDTH):
            xc = xc + convw_ref[j:j + 1, :] * xp[off + j:off + j + rows]
        conv_carry[...] = xp[rows:rows + SUBLANES]
        a, u = _lru_gates(xc, wg_ref, br_ref, bi_ref, lam_ref)
        row = lax.broadcasted_iota(jnp.int32, (rows, D_LRU), 0) % SUBLANES
        d = 1
        while d < SUBLANES:
            valid = row >= d
            u = jnp.where(valid, a * pltpu.roll(u, d, axis=0) + u, u)
            a = jnp.where(valid, a * pltpu.roll(a, d, axis=0), a)
            d *= 2
        h = h_carry[...]
        for g in range(rows // SUBLANES):
            sl = slice(g * SUBLANES, (g + 1) * SUBLANES)
            hg = a[sl] * h + u[sl]
            if store_y:
                ylg_ref[sl, :] = hg * _silu(glru[sl])
            h = jnp.broadcast_to(hg[SUBLANES - 1:SUBLANES, :], (SUBLANES, D_LRU))
        h_carry[...] = h

    @pl.when(step == 0)
    def _():
        r = lax.broadcasted_iota(jnp.int32, (tr, LANES), 0).astype(F32)
        ang = r * inv_ref[...]
        cos_tab[...] = jnp.cos(ang)
        sin_tab[...] = jnp.sin(ang)
        conv_carry[...] = jnp.zeros_like(conv_carry)
        h_carry[...] = jnp.zeros_like(h_carry)
        xm = jnp.concatenate([meta_ref[...], jnp.zeros((LANES - N_META, meta_ref.shape[1]), F32)], axis=0)
        u, _, ks, zv = project(xm, jnp.zeros((), jnp.int32), LANES)
        for h in range(N_ATT_HEADS):
            sl = slice(h * HEAD_V, (h + 1) * HEAD_V)
            kmatt_ref[h] = ks[h].astype(BF16)
            vtm_ref[h] = zv[:, sl].T.astype(BF16)
        stage_rows(kmbuf, vmbuf, ks, zv, N_META)
        for cp in meta_copies():
            cp.start()
        um = u[0:N_META]
        xb = jnp.dot(um, win_ref[:, 4 * D_ATT:4 * D_ATT + D_LRU], preferred_element_type=F32)
        lru(xb, None, N_META, False)

    u, qs, ks, zv = project(x_ref[...], N_META + step * tr, tr)
    for h in range(N_ATT_HEADS):
        sl = slice(h * HEAD_V, (h + 1) * HEAD_V)
        q_ref[h] = qs[h].astype(BF16)
        katt_ref[h] = ks[h].astype(BF16)
        vt_ref[h] = zv[:, sl].T.astype(BF16)
    slot = step % 2

    @pl.when(step >= 2)
    def _():
        for cp in row_copies(slot, step - 2):
            cp.wait()

    stage_rows(kbuf.at[slot], vbuf.at[slot], ks, zv, tr)
    for cp in row_copies(slot, step):
        cp.start()
    gatt_ref[...] = jnp.dot(u, win_ref[:, 3 * D_ATT:4 * D_ATT], preferred_element_type=F32)
    xb = jnp.dot(u, win_ref[:, 4 * D_ATT:4 * D_ATT + D_LRU], preferred_element_type=F32)
    glru = jnp.dot(u, win_ref[:, 4 * D_ATT + D_LRU:], preferred_element_type=F32)
    lru(xb, glru, tr, True)
    hlast_ref[...] = h_carry[...]
    convlast_ref[...] = conv_carry[...]

    @pl.when(step == pl.num_programs(0) - 1)
    def _():
        for cp in meta_copies():
            cp.wait()
        for cp in row_copies(slot, step):
            cp.wait()

        @pl.when(step >= 1)
        def _():
            for cp in row_copies(1 - slot, step - 1):
                cp.wait()


def _prompt_pre(x, meta, npre, win, inv, convw, convb, wg, br, bi, lam, *, tr):
    s, d = x.shape
    nt = s // tr
    full = lambda a: pl.BlockSpec(a.shape, lambda i: (0,) * a.ndim)
    out_shape = (
        jax.ShapeDtypeStruct((N_ATT_HEADS, s, HEAD_V), BF16),
        jax.ShapeDtypeStruct((N_ATT_HEADS, s, HEAD_V), BF16),
        jax.ShapeDtypeStruct((N_ATT_HEADS, HEAD_V, s), BF16),
        jax.ShapeDtypeStruct((s, D_ATT), F32),
        jax.ShapeDtypeStruct((s, D_LRU), F32),
        jax.ShapeDtypeStruct((N_ATT_HEADS, LANES, HEAD_V), BF16),
        jax.ShapeDtypeStruct((N_ATT_HEADS, HEAD_V, LANES), BF16),
        jax.ShapeDtypeStruct((SUBLANES, D_LRU), F32),
        jax.ShapeDtypeStruct((SUBLANES, D_LRU), F32),
        jax.ShapeDtypeStruct(((s + N_META) * N_ATT_HEADS, HEAD_V), F32),
        jax.ShapeDtypeStruct(((s + N_META) * N_ATT_HEADS, HEAD_V), F32),
    )
    out_specs = (
        pl.BlockSpec((N_ATT_HEADS, tr, HEAD_V), lambda i: (0, i, 0)),
        pl.BlockSpec((N_ATT_HEADS, tr, HEAD_V), lambda i: (0, i, 0)),
        pl.BlockSpec((N_ATT_HEADS, HEAD_V, tr), lambda i: (0, 0, i)),
        pl.BlockSpec((tr, D_ATT), lambda i: (i, 0)),
        pl.BlockSpec((tr, D_LRU), lambda i: (i, 0)),
    ) + tuple(pl.BlockSpec(o.shape, lambda i, n=len(o.shape): (0,) * n) for o in out_shape[5:9]) + (
        pl.BlockSpec(memory_space=pl.ANY), pl.BlockSpec(memory_space=pl.ANY))
    return pl.pallas_call(
        functools.partial(_prompt_pre_kernel, tr=tr),
        grid=(nt,),
        in_specs=[pl.BlockSpec((tr, d), lambda i: (i, 0))] + [full(a) for a in (meta, npre, win, inv, convw, convb, wg, br, bi, lam)],
        out_specs=out_specs,
        out_shape=out_shape,
        scratch_shapes=[pltpu.VMEM((tr, LANES), F32), pltpu.VMEM((tr, LANES), F32),
                        pltpu.VMEM((SUBLANES, D_LRU), F32), pltpu.VMEM((SUBLANES, D_LRU), F32),
                        pltpu.VMEM((2, tr * N_ATT_HEADS, HEAD_V), F32), pltpu.VMEM((2, tr * N_ATT_HEADS, HEAD_V), F32),
                        pltpu.VMEM((N_META * N_ATT_HEADS, HEAD_V), F32), pltpu.VMEM((N_META * N_ATT_HEADS, HEAD_V), F32),
                        pltpu.SemaphoreType.DMA((2, 2)), pltpu.SemaphoreType.DMA((2,))],
        compiler_params=pltpu.CompilerParams(dimension_semantics=("arbitrary",), vmem_limit_bytes=VMEM_LIMIT),
        name="prompt_pre",
    )(x, meta, npre, win, inv, convw, convb, wg, br, bi, lam)


DECODE_PAGES_PER_REGION = 8
DECODE_RING = 32


def _n_chunks(i, tq, tk):
    return (i * tq) // tk + 1


def _attn_kernel(pt_ref, lq1, lk1, lq2, lk2, q_ref, k_ref, vt_ref, kmeta_ref, vtmeta_ref,
                 qs_ref, kn_ref, vn_ref, ck_hbm, cv_hbm, o_ref, os_ref,
                 m_sc, l_sc, acc_sc, sa_sc, sb_sc, cma_sc, cmb_sc, kring, vring, sems, g_sc, dm_sc, dl_sc, dacc_sc,
                 *, tq, tk, n_pages, total_pages):
    h, i = pl.program_id(0), pl.program_id(1)
    pp, ring, nh = DECODE_PAGES_PER_REGION, DECODE_RING, N_ATT_HEADS
    n_unmasked = _n_chunks(i, tq, tk) - 1
    lam = _diff_lambda(lq1, lk1, lq2, lk2)

    def page_copies(flat, slot):
        page = pt_ref[jnp.minimum(flat, total_pages - 1)]
        return (pltpu.make_async_copy(ck_hbm.at[page], kring.at[slot], sems.at[0, slot]),
                pltpu.make_async_copy(cv_hbm.at[page], vring.at[slot], sems.at[1, slot]))

    @pl.when((h == 0) & (i == 0))
    def _():
        g_sc[0] = 0
        dm_sc[...] = jnp.zeros_like(dm_sc)
        dl_sc[...] = jnp.zeros_like(dl_sc)
        dacc_sc[...] = jnp.zeros_like(dacc_sc)
        for t in range(ring):
            for cp in page_copies(t, t):
                cp.start()

    def decode_wait(g, n):
        for t in range(n):
            for cp in page_copies(g + t, (g + t) % ring):
                cp.wait()

    def decode_start(g, n):
        for t in range(n):
            for cp in page_copies(g + ring + t, (g + t) % ring):
                cp.start()
        g_sc[0] = g + n

    def decode_begin(g):
        gc = jnp.minimum(g, total_pages - pp)
        d = dict(active=g < total_pages,
                 b=gc // n_pages, first=(gc % n_pages) == 0, slot0=g % ring, s=[], pv=None)
        q4 = qs_ref[d["b"]]
        lane = lax.broadcasted_iota(jnp.int32, q4.shape, 1)
        q1, q2 = jnp.where(lane < HEAD_DIM, q4, 0.0), jnp.where(lane >= HEAD_DIM, q4, 0.0)
        d["qrows"] = jnp.concatenate([q1, q2], axis=0).astype(BF16)
        kn = kn_ref[d["b"]]
        d["m0"] = jnp.concatenate([jnp.sum(q1 * kn, axis=1, keepdims=True),
                                   jnp.sum(q2 * kn, axis=1, keepdims=True)], axis=0)
        return d

    def decode_scores(d, t):
        kpage = kring[d["slot0"] + t].astype(BF16)
        d["s"].append(lax.dot_general(d["qrows"], kpage, (((1,), (1,)), ((), ())),
                                      preferred_element_type=F32))

    def decode_softmax(d):
        s, first = jnp.concatenate(d["s"], axis=1), d["first"]
        head_of_col = lax.broadcasted_iota(jnp.int32, s.shape, 1) % nh
        head_of_row = lax.broadcasted_iota(jnp.int32, s.shape, 0) % nh
        s = jnp.where(head_of_col == head_of_row, s, -jnp.inf)
        d["m_old"] = jnp.where(first, d["m0"], dm_sc[:, 0:1])
        d["l_old"] = jnp.where(first, 1.0, dl_sc[:, 0:1])
        d["m_new"] = jnp.maximum(d["m_old"], jnp.max(s, axis=1, keepdims=True))
        d["alpha"] = jnp.exp2(d["m_old"] - d["m_new"])
        p = jnp.exp2(s - d["m_new"])
        d["l_new"] = d["alpha"] * d["l_old"] + jnp.sum(p, axis=1, keepdims=True)
        d["p"] = p.astype(BF16)

    def decode_values(d, t):
        prow = kring.shape[1]
        pv = jnp.dot(d["p"][:, t * prow:(t + 1) * prow], vring[d["slot0"] + t].astype(BF16),
                     preferred_element_type=F32)
        d["pv"] = pv if d["pv"] is None else d["pv"] + pv

    def decode_finish(d):
        active, first, b = d["active"], d["first"], d["b"]
        vn = vn_ref[b]
        acc_old = jnp.where(first, jnp.concatenate([vn, vn], axis=0), dacc_sc[...])
        acc_new = d["alpha"] * acc_old + d["pv"]
        m_keep = jnp.where(active, d["m_new"], d["m_old"])
        l_keep = jnp.where(active, d["l_new"], d["l_old"])
        acc_keep = jnp.where(active, acc_new, acc_old)
        dm_sc[...] = jnp.broadcast_to(m_keep, dm_sc.shape)
        dl_sc[...] = jnp.broadcast_to(l_keep, dl_sc.shape)
        dacc_sc[...] = acc_keep
        o = acc_keep / l_keep
        os_ref[b] = o[0:nh] - lam * o[nh:2 * nh]

    qt = q_ref[0].astype(F32).T
    rows = lax.broadcasted_iota(jnp.int32, qt.shape, 0)
    qb = jnp.concatenate([jnp.where(rows < HEAD_DIM, qt, 0.0), jnp.where(rows >= HEAD_DIM, qt, 0.0)],
                         axis=1).astype(BF16)

    cb = 2 * LANES
    n_cb = 2 * tq // cb
    half = n_cb // 2
    assert n_cb % 2 == 0 and pp % half == 0
    ppb = pp // half

    def score_slab(c, j):
        start = pl.multiple_of(c * tk, tk)
        return jnp.dot(k_ref[0, pl.ds(start, tk), :], qb[:, j * cb:(j + 1) * cb],
                       preferred_element_type=F32)

    def causal_slab(j):
        kpos = n_unmasked * tk + lax.broadcasted_iota(jnp.int32, (tk, cb), 0)
        qpos = i * tq + (j * cb + lax.broadcasted_iota(jnp.int32, (tk, cb), 1)) % tq
        return kpos <= qpos

    def region(cur, c_cur, causal, nxt, c_nxt, g):
        cur_ref, cur_cm = cur
        vtt = vt_ref[0, :, pl.ds(pl.multiple_of(c_cur * tk, tk), tk)]
        d = decode_begin(g)
        for j in range(n_cb):
            sl = slice(j * cb, (j + 1) * cb)
            if j < half:
                for u in range(ppb):
                    decode_scores(d, j * ppb + u)
            if nxt is not None:
                s_nxt = score_slab(c_nxt, j)
                nxt[0][:, sl] = s_nxt
                nxt[1][:, sl] = jnp.max(s_nxt, axis=0, keepdims=True)
            if j == half:
                decode_softmax(d)
            s_cur = cur_ref[:, sl]
            if causal:
                s_cur = jnp.where(causal_slab(j), s_cur, -jnp.inf)
                cmax = jnp.max(s_cur, axis=0, keepdims=True)
            else:
                cmax = cur_cm[:, sl]
            m_old = m_sc[:, sl]
            m_new = jnp.maximum(m_old, cmax)
            alpha = jnp.exp2(m_old - m_new)
            p = jnp.exp2(s_cur - m_new)
            l_sc[:, sl] = alpha * l_sc[:, sl] + jnp.sum(p, axis=0, keepdims=True)
            acc_sc[:, sl] = alpha * acc_sc[:, sl] + jnp.dot(vtt, p.astype(BF16), preferred_element_type=F32)
            m_sc[:, sl] = m_new
            if j >= half:
                for u in range(ppb):
                    decode_values(d, (j - half) * ppb + u)
        decode_finish(d)

    krow = lax.broadcasted_iota(jnp.int32, (LANES, 2 * tq), 0)
    s_meta = jnp.where(krow < N_META, jnp.dot(kmeta_ref[0], qb, preferred_element_type=F32), -jnp.inf)
    m_meta = jnp.max(s_meta, axis=0, keepdims=True)
    p_meta = jnp.exp2(s_meta - m_meta)
    m_sc[...] = m_meta
    l_sc[...] = jnp.sum(p_meta, axis=0, keepdims=True)
    acc_sc[...] = jnp.dot(vtmeta_ref[0], p_meta.astype(BF16), preferred_element_type=F32)

    buf_a, buf_b = (sa_sc, cma_sc), (sb_sc, cmb_sc)
    for j in range(n_cb):
        s0 = score_slab(0, j)
        sa_sc[:, j * cb:(j + 1) * cb] = s0
        cma_sc[:, j * cb:(j + 1) * cb] = jnp.max(s0, axis=0, keepdims=True)

    def pair(t, carry):
        c, g = 2 * t, g_sc[0]
        decode_wait(g, 2 * pp)
        region(buf_a, c, False, buf_b, c + 1, g)
        region(buf_b, c + 1, False, buf_a, c + 2, g + pp)
        decode_start(g, 2 * pp)
        return carry

    lax.fori_loop(0, n_unmasked // 2, pair, 0)

    @pl.when(n_unmasked % 2 == 1)
    def _():
        g = g_sc[0]
        decode_wait(g, 2 * pp)
        region(buf_a, n_unmasked - 1, False, buf_b, n_unmasked, g)
        region(buf_b, n_unmasked, True, None, None, g + pp)
        decode_start(g, 2 * pp)

    @pl.when(n_unmasked % 2 == 0)
    def _():
        g = g_sc[0]
        decode_wait(g, pp)
        region(buf_a, n_unmasked, True, None, None, g)
        decode_start(g, pp)

    o = acc_sc[...] * (1.0 / l_sc[...])
    o_ref[...] = (o[:, :tq] - lam * o[:, tq:]).T

    @pl.when((h == pl.num_programs(0) - 1) & (i == pl.num_programs(1) - 1))
    def _():
        g = g_sc[0]
        for t in range(ring):
            for cp in page_copies(g + t, (g + t) % ring):
                cp.wait()


def _attention(page_table, lq1, lk1, lq2, lk2, q, katt, vt, kmatt, vtm, qs, kn, vn, cache_k, cache_v, *, tq, tk):
    nh, s, _ = q.shape
    db, n_pages = page_table.shape
    _, prow, width = cache_k.shape
    total_pages = db * n_pages
    n_slices = nh * sum(_n_chunks(i, tq, tk) for i in range(s // tq))
    assert n_slices * DECODE_PAGES_PER_REGION >= total_pages, "prompt too short to carry the decode stream"
    assert n_pages % DECODE_PAGES_PER_REGION == 0 and DECODE_RING >= 4 * DECODE_PAGES_PER_REGION
    small = lambda a: pl.BlockSpec(a.shape, lambda h, i, pt: (0,) * a.ndim)
    hbm = pl.BlockSpec(memory_space=pl.ANY)
    grid_spec = pltpu.PrefetchScalarGridSpec(
        num_scalar_prefetch=1,
        grid=(nh, s // tq),
        in_specs=[small(lq1), small(lk1), small(lq2), small(lk2),
                  pl.BlockSpec((1, tq, HEAD_V), lambda h, i, pt: (h, i, 0)),
                  pl.BlockSpec((1, s, HEAD_V), lambda h, i, pt: (h, 0, 0)),
                  pl.BlockSpec((1, HEAD_V, s), lambda h, i, pt: (h, 0, 0)),
                  pl.BlockSpec((1, LANES, HEAD_V), lambda h, i, pt: (h, 0, 0)),
                  pl.BlockSpec((1, HEAD_V, LANES), lambda h, i, pt: (h, 0, 0)),
                  small(qs), small(kn), small(vn), hbm, hbm],
        out_specs=[pl.BlockSpec((tq, HEAD_V), lambda h, i, pt: (i, h)),
                   pl.BlockSpec((db, N_ATT_HEADS, width), lambda h, i, pt: (0, 0, 0))],
        scratch_shapes=[pltpu.VMEM((1, 2 * tq), F32), pltpu.VMEM((1, 2 * tq), F32),
                        pltpu.VMEM((HEAD_V, 2 * tq), F32),
                        pltpu.VMEM((tk, 2 * tq), F32), pltpu.VMEM((tk, 2 * tq), F32),
                        pltpu.VMEM((1, 2 * tq), F32), pltpu.VMEM((1, 2 * tq), F32),
                        pltpu.VMEM((DECODE_RING, prow, width), F32), pltpu.VMEM((DECODE_RING, prow, width), F32),
                        pltpu.SemaphoreType.DMA((2, DECODE_RING)), pltpu.SMEM((1,), jnp.int32),
                        pltpu.VMEM((2 * N_ATT_HEADS, LANES), F32), pltpu.VMEM((2 * N_ATT_HEADS, LANES), F32),
                        pltpu.VMEM((2 * N_ATT_HEADS, width), F32)],
    )
    return pl.pallas_call(
        functools.partial(_attn_kernel, tq=tq, tk=tk, n_pages=n_pages, total_pages=total_pages),
        grid_spec=grid_spec,
        out_shape=(jax.ShapeDtypeStruct((s, D_ATT), F32), jax.ShapeDtypeStruct((db, N_ATT_HEADS, width), F32)),
        compiler_params=pltpu.CompilerParams(dimension_semantics=("arbitrary", "arbitrary"),
                                             vmem_limit_bytes=VMEM_LIMIT),
        name="attention",
    )(page_table.reshape(-1), lq1, lk1, lq2, lk2, q, katt, vt, kmatt, vtm, qs, kn, vn, cache_k, cache_v)


def _epilogue_kernel(x_ref, o_ref, gatt_ref, ylg_ref, sub_ref, wout_ref, npost_ref, y_ref):
    parts = []
    for h in range(N_ATT_HEADS):
        sl = slice(h * HEAD_V, (h + 1) * HEAD_V)
        o = _rms(o_ref[:, sl], sub_ref[...]) * (1.0 - LAM_INIT)
        parts.append((o * _silu(gatt_ref[:, sl])).astype(BF16))
    parts.append(ylg_ref[...].astype(BF16))
    mix = jnp.dot(jnp.concatenate(parts, axis=1), wout_ref[...], preferred_element_type=F32)
    y_ref[...] = x_ref[...] + _rms(mix, npost_ref[...])


def _epilogue(x, o_att, gatt, ylg, sub, wout, npost, *, tr):
    s, d = x.shape
    full = lambda a: pl.BlockSpec(a.shape, lambda i: (0,) * a.ndim)
    rows = lambda w: pl.BlockSpec((tr, w), lambda i: (i, 0))
    return pl.pallas_call(
        _epilogue_kernel,
        grid=(s // tr,),
        in_specs=[rows(d), rows(D_ATT), rows(D_ATT), rows(D_LRU), full(sub), full(wout), full(npost)],
        out_specs=rows(d),
        out_shape=jax.ShapeDtypeStruct((s, d), F32),
        compiler_params=pltpu.CompilerParams(dimension_semantics=("arbitrary",), vmem_limit_bytes=VMEM_LIMIT),
        name="epilogue",
    )(x, o_att, gatt, ylg, sub, wout, npost)


def _sample_pre_kernel(x_ref, npre_ref, win_ref, inv_ref, convw_ref, convb_ref, wg_ref, br_ref, bi_ref,
                       lam_ref, h0_ref, sc_ref,
                       q_ref, k_ref, v_ref, gatt_ref, ylg_ref, h_ref, cnew_ref, *, past_len):
    rows = x_ref.shape[0]
    u = _rms(x_ref[...], npre_ref[...]).astype(BF16)
    z = jnp.dot(u, win_ref[...], preferred_element_type=F32)
    ang = float(past_len) * inv_ref[...]
    cos, sin = jnp.cos(ang), jnp.sin(ang)
    fh = _first_half_mask((rows, LANES))
    sin_signed = jnp.where(fh[0:1], -sin, sin)
    for h in range(N_ATT_HEADS):
        sl = slice(h * HEAD_V, (h + 1) * HEAD_V)
        q_ref[:, sl] = _rope_block(z[:, sl], cos, sin_signed, fh) * Q_SCALE
        k_ref[:, sl] = _rope_block(z[:, D_ATT + h * HEAD_V:D_ATT + (h + 1) * HEAD_V], cos, sin_signed, fh)
    v_ref[...] = z[:, 2 * D_ATT:3 * D_ATT]
    gatt_ref[...] = z[:, 3 * D_ATT:4 * D_ATT]
    xb = z[:, 4 * D_ATT:4 * D_ATT + D_LRU]
    glru = z[:, 4 * D_ATT + D_LRU:]
    xc = convb_ref[...] + convw_ref[CONV_WIDTH - 1:CONV_WIDTH, :] * xb
    for j in range(CONV_WIDTH - 1):
        xc = xc + convw_ref[j:j + 1, :] * sc_ref[:, j * D_LRU:(j + 1) * D_LRU]
    a, uu = _lru_gates(xc, wg_ref, br_ref, bi_ref, lam_ref)
    hn = a * h0_ref[...] + uu
    h_ref[...] = hn
    ylg_ref[...] = hn * _silu(glru)
    cnew_ref[:, 0:(CONV_WIDTH - 2) * D_LRU] = sc_ref[:, D_LRU:]
    cnew_ref[:, (CONV_WIDTH - 2) * D_LRU:] = xb


def _sample_pre(x, npre, win, inv, convw, convb, wg, br, bi, lam, h0, sc, *, past_len):
    rows = x.shape[0]
    out_shape = tuple(jax.ShapeDtypeStruct((rows, w), F32)
                      for w in (D_ATT, D_ATT, D_ATT, D_ATT, D_LRU, D_LRU, (CONV_WIDTH - 1) * D_LRU))
    return pl.pallas_call(
        functools.partial(_sample_pre_kernel, past_len=past_len),
        out_shape=out_shape,
        compiler_params=pltpu.CompilerParams(vmem_limit_bytes=VMEM_LIMIT),
        name="sample_pre",
    )(x, npre, win, inv, convw, convb, wg, br, bi, lam, h0, sc)


def _tile(n, target):
    t = min(n, target)
    assert n % t == 0, (n, t)
    return t


def kernel(x_prompt, x_sample, cache_k, cache_v, state_h, state_conv, page_table, meta_tokens, norm_pre, w_in, conv_w, conv_b, w_gate_r, b_gate_r, w_gate_i, b_gate_i, lru_lambda, lambda_q1, lambda_k1, lambda_q2, lambda_k2, attn_subnorm, w_out, norm_post):
    assert x_prompt.shape[0] == 1 and x_sample.shape[1] == 1 and w_in.shape[0] == 1
    seq, d_model = x_prompt.shape[1], x_prompt.shape[2]
    db = x_sample.shape[0]
    n_pool, page = cache_k.shape[1], cache_k.shape[2]
    past_len = page_table.shape[1] * page

    win = w_in[0].astype(BF16)
    wout = w_out[0].astype(BF16)
    row = lambda a: a.reshape(1, -1)
    npre, npost, sub = row(norm_pre[0]), row(norm_post[0]), row(attn_subnorm[0])
    convw, convb = conv_w[0], row(conv_b[0])
    br, bi, lam = row(b_gate_r[0]), row(b_gate_i[0]), row(lru_lambda[0])
    blocks = jnp.arange(D_LRU) // (D_LRU // N_LRU_BLOCKS)
    same_block = blocks[:, None] == blocks[None, :]
    bdiag = lambda w: jnp.where(same_block, jnp.tile(w.reshape(D_LRU, -1), (1, N_LRU_BLOCKS)), 0.0)
    wg = jnp.concatenate([bdiag(w_gate_r[0]), bdiag(w_gate_i[0])], axis=1).astype(BF16)
    lq1, lk1, lq2, lk2 = row(lambda_q1[0]), row(lambda_k1[0]), row(lambda_q2[0]), row(lambda_k2[0])
    inv = ROPE_THETA ** (-jnp.arange(0, HEAD_DIM, 2, dtype=F32) / HEAD_DIM)
    inv = jnp.tile(inv, LANES // (HEAD_DIM // 2)).reshape(1, LANES)

    xp, xs = x_prompt[0], x_sample[:, 0]
    tr = _tile(seq, 512)
    (q, katt, vt, gatt, ylg, kmatt, vtm, hlast, convlast, k_rows, v_rows) = _prompt_pre(
        xp, meta_tokens, npre, win, inv, convw, convb, wg, br, bi, lam, tr=tr)
    qs, ks, vs, gatt_s, ylg_s, h_s, conv_s = _sample_pre(
        xs, npre, win, inv, convw, convb, wg, br, bi, lam, state_h[0],
        state_conv[0].reshape(db, (CONV_WIDTH - 1) * D_LRU), past_len=past_len)

    per_head = lambda a: a.reshape(db, N_ATT_HEADS, HEAD_V)
    o_att, o_s = _attention(page_table, lq1, lk1, lq2, lk2, q, katt, vt, kmatt, vtm,
                            per_head(qs), per_head(ks), per_head(vs),
                            cache_k[0].reshape(n_pool, page * N_ATT_HEADS, HEAD_V),
                            cache_v[0].reshape(n_pool, page * N_ATT_HEADS, HEAD_V),
                            tq=_tile(seq, 512), tk=_tile(seq, 1024))

    y_prompt = _epilogue(xp, o_att, gatt, ylg, sub, wout, npost, tr=tr)
    y_sample = _epilogue(xs, o_s.reshape(db, D_ATT), gatt_s, ylg_s, sub, wout, npost, tr=db)

    t_all = seq + N_META
    k_prompt = k_rows.reshape(1, 1, t_all, N_ATT_HEADS, HEAD_V)
    v_prompt = v_rows.reshape(1, 1, t_all, N_ATT_HEADS, HEAD_V)
    return (y_prompt[None], y_sample[:, None], k_prompt, v_prompt,
            hlast[0:1][None], convlast[SUBLANES - (CONV_WIDTH - 1):][None, None],
            ks.reshape(1, db, 1, N_ATT_HEADS, HEAD_V), vs.reshape(1, db, 1, N_ATT_HEADS, HEAD_V),
            h_s[None], conv_s.reshape(1, db, CONV_WIDTH - 1, D_LRU))
```

```python
import functools
import math

import jax
import jax.numpy as jnp
from jax import lax
from jax.experimental import pallas as pl
from jax.experimental.pallas import tpu as pltpu

N_META = 16
N_ATT_HEADS = 4
HEAD_DIM = 64
HEAD_V = 2 * HEAD_DIM
D_ATT = N_ATT_HEADS * HEAD_V
D_LRU = 512
N_LRU_BLOCKS = 8
CONV_WIDTH = 4
LRU_C = 8.0
ROPE_THETA = 10000.0
EPS = 1e-6
LAM_INIT = 0.8 - 0.6 * math.exp(-0.3 * 0)
Q_SCALE = HEAD_DIM ** -0.5 * math.log2(math.e)

LANES = 128
SUBLANES = 8
VMEM_LIMIT = 56 * 1024 * 1024
BF16_ROWS = 16

F32 = jnp.float32
BF16 = jnp.bfloat16


def _rms(x, g):
    return x * lax.rsqrt(jnp.mean(x * x, axis=-1, keepdims=True) + EPS) * g


def _silu(x):
    return x * jax.nn.sigmoid(x)


def _first_half_mask(shape):
    lane = lax.broadcasted_iota(jnp.int32, shape, len(shape) - 1)
    return (lane % HEAD_DIM) < (HEAD_DIM // 2)


def _rope_block(zb, cos, sin_signed, first_half):
    half = HEAD_DIM // 2
    partner = jnp.where(first_half, pltpu.roll(zb, LANES - half, axis=1), pltpu.roll(zb, half, axis=1))
    return zb * cos + partner * sin_signed


def _diff_lambda(lq1, lk1, lq2, lk2):
    return (jnp.exp(jnp.sum(lq1[...] * lk1[...])) - jnp.exp(jnp.sum(lq2[...] * lk2[...])) + LAM_INIT)


def _lru_gates(xc, wg_ref, br_ref, bi_ref, lam_ref):
    gates = jnp.dot(xc.astype(BF16), wg_ref[...], preferred_element_type=F32)
    r = jax.nn.sigmoid(gates[:, :D_LRU] + br_ref[...])
    i = jax.nn.sigmoid(gates[:, D_LRU:] + bi_ref[...])
    neg_lam = -lam_ref[...]
    softplus = jnp.maximum(neg_lam, 0.0) + jnp.log1p(jnp.exp(-jnp.abs(neg_lam)))
    log_a = -LRU_C * r * softplus
    a = jnp.exp(log_a)
    one_minus_a2 = 1.0 - jnp.exp(2.0 * log_a)
    root = jnp.where(one_minus_a2 > 0.0, one_minus_a2 * lax.rsqrt(one_minus_a2), 0.0)
    u = root * (i * xc)
    return a, u


def _prompt_pre_kernel(x_ref, meta_ref, npre_ref, win_ref, inv_ref, convw_ref, convb_ref, wg_ref,
                       br_ref, bi_ref, lam_ref,
                       q_ref, katt_ref, vt_ref, gatt_ref, ylg_ref,
                       kmatt_ref, vtm_ref, hlast_ref, convlast_ref, k_hbm, v_hbm,
                       cos_tab, sin_tab, conv_carry, h_carry, kbuf, vbuf, kmbuf, vmbuf, sems, msems, *, tr):
    step = pl.program_id(0)
    nh = N_ATT_HEADS

    def row_copies(slot, tile):
        rows = pl.ds((N_META + tile * tr) * nh, tr * nh)
        return (pltpu.make_async_copy(kbuf.at[slot], k_hbm.at[rows], sems.at[0, slot]),
                pltpu.make_async_copy(vbuf.at[slot], v_hbm.at[rows], sems.at[1, slot]))

    def meta_copies():
        rows = pl.ds(0, N_META * nh)
        return (pltpu.make_async_copy(kmbuf, k_hbm.at[rows], msems.at[0]),
                pltpu.make_async_copy(vmbuf, v_hbm.at[rows], msems.at[1]))

    def stage_rows(kdst, vdst, ks, zv, rows):
        for h in range(nh):
            kdst[pl.ds(h, rows, stride=nh), :] = ks[h][0:rows]
            vdst[pl.ds(h, rows, stride=nh), :] = zv[0:rows, h * HEAD_V:(h + 1) * HEAD_V]

    def rope_tables(pos0, rows):
        base = pos0.astype(F32) * inv_ref[...]
        cb, sb = jnp.cos(base), jnp.sin(base)
        cr, sr = cos_tab[0:rows, :], sin_tab[0:rows, :]
        cos = cr * cb - sr * sb
        sin = sr * cb + cr * sb
        return cos, jnp.where(_first_half_mask((rows, LANES)), -sin, sin)

    def project(x, pos0, rows):
        u = _rms(x, npre_ref[...]).astype(BF16)
        cos, sin_signed = rope_tables(pos0, rows)
        fh = _first_half_mask((rows, LANES))
        zq = jnp.dot(u, win_ref[:, 0:D_ATT], preferred_element_type=F32)
        zk = jnp.dot(u, win_ref[:, D_ATT:2 * D_ATT], preferred_element_type=F32)
        zv = jnp.dot(u, win_ref[:, 2 * D_ATT:3 * D_ATT], preferred_element_type=F32)
        qs, ks = [], []
        for h in range(N_ATT_HEADS):
            sl = slice(h * HEAD_V, (h + 1) * HEAD_V)
            qs.append(_rope_block(zq[:, sl], cos, sin_signed, fh) * Q_SCALE)
            ks.append(_rope_block(zk[:, sl], cos, sin_signed, fh))
        return u, qs, ks, zv

    def lru(xb, glru, rows, store_y):
        xp = jnp.concatenate([conv_carry[...], xb], axis=0)
        off = SUBLANES - (CONV_WIDTH - 1)
        xc = convb_ref[...] + convw_ref[0:1, :] * xp[off:off + rows]
        for j in range(1, CONV_WIDTH):
            xc = xc + convw_ref[j:j + 1, :] * xp[off + j:off + j + rows]
        conv_carry[...] = xp[rows:rows + SUBLANES]
        a, u = _lru_gates(xc, wg_ref, br_ref, bi_ref, lam_ref)
        row = lax.broadcasted_iota(jnp.int32, (rows, D_LRU), 0) % SUBLANES
        d = 1
        while d < SUBLANES:
            valid = row >= d
            u = jnp.where(valid, a * pltpu.roll(u, d, axis=0) + u, u)
            a = jnp.where(valid, a * pltpu.roll(a, d, axis=0), a)
            d *= 2
        h = h_carry[...]
        pending = []
        for g in range(rows // SUBLANES):
            sl = slice(g * SUBLANES, (g + 1) * SUBLANES)
            hg = a[sl] * h + u[sl]
            if store_y:
                pending.append(hg * _silu(glru[sl]))
                if len(pending) == BF16_ROWS // SUBLANES:
                    lo = (g + 1) * SUBLANES - BF16_ROWS
                    ylg_ref[lo:lo + BF16_ROWS, :] = jnp.concatenate(pending, axis=0).astype(BF16)
                    pending = []
            h = jnp.broadcast_to(hg[SUBLANES - 1:SUBLANES, :], (SUBLANES, D_LRU))
        assert not pending
        h_carry[...] = h

    @pl.when(step == 0)
    def _():
        r = lax.broadcasted_iota(jnp.int32, (tr, LANES), 0).astype(F32)
        ang = r * inv_ref[...]
        cos_tab[...] = jnp.cos(ang)
        sin_tab[...] = jnp.sin(ang)
        conv_carry[...] = jnp.zeros_like(conv_carry)
        h_carry[...] = jnp.zeros_like(h_carry)
        xm = jnp.concatenate([meta_ref[...], jnp.zeros((LANES - N_META, meta_ref.shape[1]), F32)], axis=0)
        u, _, ks, zv = project(xm, jnp.zeros((), jnp.int32), LANES)
        for h in range(N_ATT_HEADS):
            sl = slice(h * HEAD_V, (h + 1) * HEAD_V)
            kmatt_ref[h] = ks[h].astype(BF16)
            vtm_ref[h] = zv[:, sl].T.astype(BF16)
        stage_rows(kmbuf, vmbuf, ks, zv, N_META)
        for cp in meta_copies():
            cp.start()
        um = u[0:N_META]
        xb = jnp.dot(um, win_ref[:, 4 * D_ATT:4 * D_ATT + D_LRU], preferred_element_type=F32)
        lru(xb, None, N_META, False)

    u, qs, ks, zv = project(x_ref[...], N_META + step * tr, tr)
    for h in range(N_ATT_HEADS):
        sl = slice(h * HEAD_V, (h + 1) * HEAD_V)
        q_ref[h] = qs[h].astype(BF16)
        katt_ref[h] = ks[h].astype(BF16)
        vt_ref[h] = zv[:, sl].T.astype(BF16)
    slot = step % 2

    @pl.when(step >= 2)
    def _():
        for cp in row_copies(slot, step - 2):
            cp.wait()

    stage_rows(kbuf.at[slot], vbuf.at[slot], ks, zv, tr)
    for cp in row_copies(slot, step):
        cp.start()
    gatt_ref[...] = jnp.dot(u, win_ref[:, 3 * D_ATT:4 * D_ATT], preferred_element_type=F32)
    xb = jnp.dot(u, win_ref[:, 4 * D_ATT:4 * D_ATT + D_LRU], preferred_element_type=F32)
    glru = jnp.dot(u, win_ref[:, 4 * D_ATT + D_LRU:], preferred_element_type=F32)
    lru(xb, glru, tr, True)
    hlast_ref[...] = h_carry[...]
    convlast_ref[...] = conv_carry[...]

    @pl.when(step == pl.num_programs(0) - 1)
    def _():
        for cp in meta_copies():
            cp.wait()
        for cp in row_copies(slot, step):
            cp.wait()

        @pl.when(step >= 1)
        def _():
            for cp in row_copies(1 - slot, step - 1):
                cp.wait()


def _prompt_pre(x, meta, npre, win, inv, convw, convb, wg, br, bi, lam, *, tr):
    s, d = x.shape
    nt = s // tr
    full = lambda a: pl.BlockSpec(a.shape, lambda i: (0,) * a.ndim)
    out_shape = (
        jax.ShapeDtypeStruct((N_ATT_HEADS, s, HEAD_V), BF16),
        jax.ShapeDtypeStruct((N_ATT_HEADS, s, HEAD_V), BF16),
        jax.ShapeDtypeStruct((N_ATT_HEADS, HEAD_V, s), BF16),
        jax.ShapeDtypeStruct((s, D_ATT), F32),
        jax.ShapeDtypeStruct((s, D_LRU), BF16),
        jax.ShapeDtypeStruct((N_ATT_HEADS, LANES, HEAD_V), BF16),
        jax.ShapeDtypeStruct((N_ATT_HEADS, HEAD_V, LANES), BF16),
        jax.ShapeDtypeStruct((SUBLANES, D_LRU), F32),
        jax.ShapeDtypeStruct((SUBLANES, D_LRU), F32),
        jax.ShapeDtypeStruct(((s + N_META) * N_ATT_HEADS, HEAD_V), F32),
        jax.ShapeDtypeStruct(((s + N_META) * N_ATT_HEADS, HEAD_V), F32),
    )
    out_specs = (
        pl.BlockSpec((N_ATT_HEADS, tr, HEAD_V), lambda i: (0, i, 0)),
        pl.BlockSpec((N_ATT_HEADS, tr, HEAD_V), lambda i: (0, i, 0)),
        pl.BlockSpec((N_ATT_HEADS, HEAD_V, tr), lambda i: (0, 0, i)),
        pl.BlockSpec((tr, D_ATT), lambda i: (i, 0)),
        pl.BlockSpec((tr, D_LRU), lambda i: (i, 0)),
    ) + tuple(pl.BlockSpec(o.shape, lambda i, n=len(o.shape): (0,) * n) for o in out_shape[5:9]) + (
        pl.BlockSpec(memory_space=pl.ANY), pl.BlockSpec(memory_space=pl.ANY))
    return pl.pallas_call(
        functools.partial(_prompt_pre_kernel, tr=tr),
        grid=(nt,),
        in_specs=[pl.BlockSpec((tr, d), lambda i: (i, 0))] + [full(a) for a in (meta, npre, win, inv, convw, convb, wg, br, bi, lam)],
        out_specs=out_specs,
        out_shape=out_shape,
        scratch_shapes=[pltpu.VMEM((tr, LANES), F32), pltpu.VMEM((tr, LANES), F32),
                        pltpu.VMEM((SUBLANES, D_LRU), F32), pltpu.VMEM((SUBLANES, D_LRU), F32),
                        pltpu.VMEM((2, tr * N_ATT_HEADS, HEAD_V), F32), pltpu.VMEM((2, tr * N_ATT_HEADS, HEAD_V), F32),
                        pltpu.VMEM((N_META * N_ATT_HEADS, HEAD_V), F32), pltpu.VMEM((N_META * N_ATT_HEADS, HEAD_V), F32),
                        pltpu.SemaphoreType.DMA((2, 2)), pltpu.SemaphoreType.DMA((2,))],
        compiler_params=pltpu.CompilerParams(dimension_semantics=("arbitrary",), vmem_limit_bytes=VMEM_LIMIT),
        name="prompt_pre",
    )(x, meta, npre, win, inv, convw, convb, wg, br, bi, lam)


DECODE_PAGES_PER_REGION = 8
DECODE_RING = 32


def _n_chunks(i, tq, tk):
    return (i * tq) // tk + 1


def _attn_kernel(pt_ref, lq1, lk1, lq2, lk2, q_ref, k_ref, vt_ref, kmeta_ref, vtmeta_ref,
                 qs_ref, kn_ref, vn_ref, ck_hbm, cv_hbm, o_ref, os_ref,
                 m_sc, l_sc, acc_sc, sa_sc, sb_sc, cma_sc, cmb_sc, kring, vring, sems, g_sc, dm_sc, dl_sc, dacc_sc,
                 *, tq, tk, n_pages, total_pages):
    h, i = pl.program_id(0), pl.program_id(1)
    pp, ring, nh = DECODE_PAGES_PER_REGION, DECODE_RING, N_ATT_HEADS
    n_unmasked = _n_chunks(i, tq, tk) - 1
    lam = _diff_lambda(lq1, lk1, lq2, lk2)

    def page_copies(flat, slot):
        page = pt_ref[jnp.minimum(flat, total_pages - 1)]
        return (pltpu.make_async_copy(ck_hbm.at[page], kring.at[slot], sems.at[0, slot]),
                pltpu.make_async_copy(cv_hbm.at[page], vring.at[slot], sems.at[1, slot]))

    @pl.when((h == 0) & (i == 0))
    def _():
        g_sc[0] = 0
        dm_sc[...] = jnp.zeros_like(dm_sc)
        dl_sc[...] = jnp.zeros_like(dl_sc)
        dacc_sc[...] = jnp.zeros_like(dacc_sc)
        for t in range(ring):
            for cp in page_copies(t, t):
                cp.start()

    def decode_wait(g, n):
        for t in range(n):
            for cp in page_copies(g + t, (g + t) % ring):
                cp.wait()

    def decode_start(g, n):
        for t in range(n):
            for cp in page_copies(g + ring + t, (g + t) % ring):
                cp.start()
        g_sc[0] = g + n

    def decode_begin(g):
        gc = jnp.minimum(g, total_pages - pp)
        d = dict(active=g < total_pages,
                 b=gc // n_pages, first=(gc % n_pages) == 0, slot0=g % ring, s=[], pv=None)
        q4 = qs_ref[d["b"]]
        lane = lax.broadcasted_iota(jnp.int32, q4.shape, 1)
        q1, q2 = jnp.where(lane < HEAD_DIM, q4, 0.0), jnp.where(lane >= HEAD_DIM, q4, 0.0)
        d["qrows"] = jnp.concatenate([q1, q2], axis=0).astype(BF16)
        kn = kn_ref[d["b"]]
        d["m0"] = jnp.concatenate([jnp.sum(q1 * kn, axis=1, keepdims=True),
                                   jnp.sum(q2 * kn, axis=1, keepdims=True)], axis=0)
        return d

    def decode_scores(d, t):
        kpage = kring[d["slot0"] + t].astype(BF16)
        d["s"].append(lax.dot_general(d["qrows"], kpage, (((1,), (1,)), ((), ())),
                                      preferred_element_type=F32))

    def decode_softmax(d):
        s, first = jnp.concatenate(d["s"], axis=1), d["first"]
        head_of_col = lax.broadcasted_iota(jnp.int32, s.shape, 1) % nh
        head_of_row = lax.broadcasted_iota(jnp.int32, s.shape, 0) % nh
        s = jnp.where(head_of_col == head_of_row, s, -jnp.inf)
        d["m_old"] = jnp.where(first, d["m0"], dm_sc[:, 0:1])
        d["l_old"] = jnp.where(first, 1.0, dl_sc[:, 0:1])
        d["m_new"] = jnp.maximum(d["m_old"], jnp.max(s, axis=1, keepdims=True))
        d["alpha"] = jnp.exp2(d["m_old"] - d["m_new"])
        p = jnp.exp2(s - d["m_new"])
        d["l_new"] = d["alpha"] * d["l_old"] + jnp.sum(p, axis=1, keepdims=True)
        d["p"] = p.astype(BF16)

    def decode_values(d, t):
        prow = kring.shape[1]
        pv = jnp.dot(d["p"][:, t * prow:(t + 1) * prow], vring[d["slot0"] + t].astype(BF16),
                     preferred_element_type=F32)
        d["pv"] = pv if d["pv"] is None else d["pv"] + pv

    def decode_finish(d):
        active, first, b = d["active"], d["first"], d["b"]
        vn = vn_ref[b]
        acc_old = jnp.where(first, jnp.concatenate([vn, vn], axis=0), dacc_sc[...])
        acc_new = d["alpha"] * acc_old + d["pv"]
        m_keep = jnp.where(active, d["m_new"], d["m_old"])
        l_keep = jnp.where(active, d["l_new"], d["l_old"])
        acc_keep = jnp.where(active, acc_new, acc_old)
        dm_sc[...] = jnp.broadcast_to(m_keep, dm_sc.shape)
        dl_sc[...] = jnp.broadcast_to(l_keep, dl_sc.shape)
        dacc_sc[...] = acc_keep
        o = acc_keep / l_keep
        os_ref[b] = o[0:nh] - lam * o[nh:2 * nh]

    qt = q_ref[0].astype(F32).T
    rows = lax.broadcasted_iota(jnp.int32, qt.shape, 0)
    qb = jnp.concatenate([jnp.where(rows < HEAD_DIM, qt, 0.0), jnp.where(rows >= HEAD_DIM, qt, 0.0)],
                         axis=1).astype(BF16)

    cb = 2 * LANES
    n_cb = 2 * tq // cb
    half = n_cb // 2
    assert n_cb % 2 == 0 and pp % half == 0
    ppb = pp // half

    def score_slab(c, j):
        start = pl.multiple_of(c * tk, tk)
        return jnp.dot(k_ref[0, pl.ds(start, tk), :], qb[:, j * cb:(j + 1) * cb],
                       preferred_element_type=F32)

    def causal_slab(j):
        kpos = n_unmasked * tk + lax.broadcasted_iota(jnp.int32, (tk, cb), 0)
        qpos = i * tq + (j * cb + lax.broadcasted_iota(jnp.int32, (tk, cb), 1)) % tq
        return kpos <= qpos

    def region(cur, c_cur, causal, nxt, c_nxt, g):
        cur_ref, cur_cm = cur
        vtt = vt_ref[0, :, pl.ds(pl.multiple_of(c_cur * tk, tk), tk)]
        d = decode_begin(g)
        for j in range(n_cb):
            sl = slice(j * cb, (j + 1) * cb)
            if j < half:
                for u in range(ppb):
                    decode_scores(d, j * ppb + u)
            if nxt is not None:
                s_nxt = score_slab(c_nxt, j)
                nxt[0][:, sl] = s_nxt
                nxt[1][:, sl] = jnp.max(s_nxt, axis=0, keepdims=True)
            if j == half:
                decode_softmax(d)
            s_cur = cur_ref[:, sl]
            if causal:
                s_cur = jnp.where(causal_slab(j), s_cur, -jnp.inf)
                cmax = jnp.max(s_cur, axis=0, keepdims=True)
            else:
                cmax = cur_cm[:, sl]
            m_old = m_sc[:, sl]
            m_new = jnp.maximum(m_old, cmax)
            alpha = jnp.exp2(m_old - m_new)
            p = jnp.exp2(s_cur - m_new)
            l_sc[:, sl] = alpha * l_sc[:, sl] + jnp.sum(p, axis=0, keepdims=True)
            acc_sc[:, sl] = alpha * acc_sc[:, sl] + jnp.dot(vtt, p.astype(BF16), preferred_element_type=F32)
            m_sc[:, sl] = m_new
            if j >= half:
                for u in range(ppb):
                    decode_values(d, (j - half) * ppb + u)
        decode_finish(d)

    krow = lax.broadcasted_iota(jnp.int32, (LANES, 2 * tq), 0)
    s_meta = jnp.where(krow < N_META, jnp.dot(kmeta_ref[0], qb, preferred_element_type=F32), -jnp.inf)
    m_meta = jnp.max(s_meta, axis=0, keepdims=True)
    p_meta = jnp.exp2(s_meta - m_meta)
    m_sc[...] = m_meta
    l_sc[...] = jnp.sum(p_meta, axis=0, keepdims=True)
    acc_sc[...] = jnp.dot(vtmeta_ref[0], p_meta.astype(BF16), preferred_element_type=F32)

    buf_a, buf_b = (sa_sc, cma_sc), (sb_sc, cmb_sc)
    for j in range(n_cb):
        s0 = score_slab(0, j)
        sa_sc[:, j * cb:(j + 1) * cb] = s0
        cma_sc[:, j * cb:(j + 1) * cb] = jnp.max(s0, axis=0, keepdims=True)

    def pair(t, carry):
        c, g = 2 * t, g_sc[0]
        decode_wait(g, 2 * pp)
        region(buf_a, c, False, buf_b, c + 1, g)
        region(buf_b, c + 1, False, buf_a, c + 2, g + pp)
        decode_start(g, 2 * pp)
        return carry

    lax.fori_loop(0, n_unmasked // 2, pair, 0)

    @pl.when(n_unmasked % 2 == 1)
    def _():
        g = g_sc[0]
        decode_wait(g, 2 * pp)
        region(buf_a, n_unmasked - 1, False, buf_b, n_unmasked, g)
        region(buf_b, n_unmasked, True, None, None, g + pp)
        decode_start(g, 2 * pp)

    @pl.when(n_unmasked % 2 == 0)
    def _():
        g = g_sc[0]
        decode_wait(g, pp)
        region(buf_a, n_unmasked, True, None, None, g)
        decode_start(g, pp)

    o = acc_sc[...] * (1.0 / l_sc[...])
    o_ref[...] = (o[:, :tq] - lam * o[:, tq:]).T

    @pl.when((h == pl.num_programs(0) - 1) & (i == pl.num_programs(1) - 1))
    def _():
        g = g_sc[0]
        for t in range(ring):
            for cp in page_copies(g + t, (g + t) % ring):
                cp.wait()


def _attention(page_table, lq1, lk1, lq2, lk2, q, katt, vt, kmatt, vtm, qs, kn, vn, cache_k, cache_v, *, tq, tk):
    nh, s, _ = q.shape
    db, n_pages = page_table.shape
    _, prow, width = cache_k.shape
    total_pages = db * n_pages
    n_slices = nh * sum(_n_chunks(i, tq, tk) for i in range(s // tq))
    assert n_slices * DECODE_PAGES_PER_REGION >= total_pages, "prompt too short to carry the decode stream"
    assert n_pages % DECODE_PAGES_PER_REGION == 0 and DECODE_RING >= 4 * DECODE_PAGES_PER_REGION
    small = lambda a: pl.BlockSpec(a.shape, lambda h, i, pt: (0,) * a.ndim)
    hbm = pl.BlockSpec(memory_space=pl.ANY)
    grid_spec = pltpu.PrefetchScalarGridSpec(
        num_scalar_prefetch=1,
        grid=(nh, s // tq),
        in_specs=[small(lq1), small(lk1), small(lq2), small(lk2),
                  pl.BlockSpec((1, tq, HEAD_V), lambda h, i, pt: (h, i, 0)),
                  pl.BlockSpec((1, s, HEAD_V), lambda h, i, pt: (h, 0, 0)),
                  pl.BlockSpec((1, HEAD_V, s), lambda h, i, pt: (h, 0, 0)),
                  pl.BlockSpec((1, LANES, HEAD_V), lambda h, i, pt: (h, 0, 0)),
                  pl.BlockSpec((1, HEAD_V, LANES), lambda h, i, pt: (h, 0, 0)),
                  small(qs), small(kn), small(vn), hbm, hbm],
        out_specs=[pl.BlockSpec((tq, HEAD_V), lambda h, i, pt: (i, h)),
                   pl.BlockSpec((db, N_ATT_HEADS, width), lambda h, i, pt: (0, 0, 0))],
        scratch_shapes=[pltpu.VMEM((1, 2 * tq), F32), pltpu.VMEM((1, 2 * tq), F32),
                        pltpu.VMEM((HEAD_V, 2 * tq), F32),
                        pltpu.VMEM((tk, 2 * tq), F32), pltpu.VMEM((tk, 2 * tq), F32),
                        pltpu.VMEM((1, 2 * tq), F32), pltpu.VMEM((1, 2 * tq), F32),
                        pltpu.VMEM((DECODE_RING, prow, width), F32), pltpu.VMEM((DECODE_RING, prow, width), F32),
                        pltpu.SemaphoreType.DMA((2, DECODE_RING)), pltpu.SMEM((1,), jnp.int32),
                        pltpu.VMEM((2 * N_ATT_HEADS, LANES), F32), pltpu.VMEM((2 * N_ATT_HEADS, LANES), F32),
                        pltpu.VMEM((2 * N_ATT_HEADS, width), F32)],
    )
    return pl.pallas_call(
        functools.partial(_attn_kernel, tq=tq, tk=tk, n_pages=n_pages, total_pages=total_pages),
        grid_spec=grid_spec,
        out_shape=(jax.ShapeDtypeStruct((s, D_ATT), F32), jax.ShapeDtypeStruct((db, N_ATT_HEADS, width), F32)),
        compiler_params=pltpu.CompilerParams(dimension_semantics=("arbitrary", "arbitrary"),
                                             vmem_limit_bytes=VMEM_LIMIT),
        name="attention",
    )(page_table.reshape(-1), lq1, lk1, lq2, lk2, q, katt, vt, kmatt, vtm, qs, kn, vn, cache_k, cache_v)


def _epilogue_kernel(x_ref, o_ref, gatt_ref, ylg_ref, sub_ref, wout_ref, npost_ref, y_ref):
    parts = []
    for h in range(N_ATT_HEADS):
        sl = slice(h * HEAD_V, (h + 1) * HEAD_V)
        o = _rms(o_ref[:, sl], sub_ref[...]) * (1.0 - LAM_INIT)
        parts.append((o * _silu(gatt_ref[:, sl])).astype(BF16))
    parts.append(ylg_ref[...].astype(BF16))
    mix = jnp.dot(jnp.concatenate(parts, axis=1), wout_ref[...], preferred_element_type=F32)
    y_ref[...] = x_ref[...] + _rms(mix, npost_ref[...])


def _epilogue(x, o_att, gatt, ylg, sub, wout, npost, *, tr):
    s, d = x.shape
    full = lambda a: pl.BlockSpec(a.shape, lambda i: (0,) * a.ndim)
    rows = lambda w: pl.BlockSpec((tr, w), lambda i: (i, 0))
    return pl.pallas_call(
        _epilogue_kernel,
        grid=(s // tr,),
        in_specs=[rows(d), rows(D_ATT), rows(D_ATT), rows(D_LRU), full(sub), full(wout), full(npost)],
        out_specs=rows(d),
        out_shape=jax.ShapeDtypeStruct((s, d), F32),
        compiler_params=pltpu.CompilerParams(dimension_semantics=("arbitrary",), vmem_limit_bytes=VMEM_LIMIT),
        name="epilogue",
    )(x, o_att, gatt, ylg, sub, wout, npost)


def _sample_pre_kernel(x_ref, npre_ref, win_ref, inv_ref, convw_ref, convb_ref, wg_ref, br_ref, bi_ref,
                       lam_ref, h0_ref, sc_ref,
                       q_ref, k_ref, v_ref, gatt_ref, ylg_ref, h_ref, cnew_ref, *, past_len):
    rows = x_ref.shape[0]
    u = _rms(x_ref[...], npre_ref[...]).astype(BF16)
    z = jnp.dot(u, win_ref[...], preferred_element_type=F32)
    ang = float(past_len) * inv_ref[...]
    cos, sin = jnp.cos(ang), jnp.sin(ang)
    fh = _first_half_mask((rows, LANES))
    sin_signed = jnp.where(fh[0:1], -sin, sin)
    for h in range(N_ATT_HEADS):
        sl = slice(h * HEAD_V, (h + 1) * HEAD_V)
        q_ref[:, sl] = _rope_block(z[:, sl], cos, sin_signed, fh) * Q_SCALE
        k_ref[:, sl] = _rope_block(z[:, D_ATT + h * HEAD_V:D_ATT + (h + 1) * HEAD_V], cos, sin_signed, fh)
    v_ref[...] = z[:, 2 * D_ATT:3 * D_ATT]
    gatt_ref[...] = z[:, 3 * D_ATT:4 * D_ATT]
    xb = z[:, 4 * D_ATT:4 * D_ATT + D_LRU]
    glru = z[:, 4 * D_ATT + D_LRU:]
    xc = convb_ref[...] + convw_ref[CONV_WIDTH - 1:CONV_WIDTH, :] * xb
    for j in range(CONV_WIDTH - 1):
        xc = xc + convw_ref[j:j + 1, :] * sc_ref[:, j * D_LRU:(j + 1) * D_LRU]
    a, uu = _lru_gates(xc, wg_ref, br_ref, bi_ref, lam_ref)
    hn = a * h0_ref[...] + uu
    h_ref[...] = hn
    ylg_ref[...] = hn * _silu(glru)
    cnew_ref[:, 0:(CONV_WIDTH - 2) * D_LRU] = sc_ref[:, D_LRU:]
    cnew_ref[:, (CONV_WIDTH - 2) * D_LRU:] = xb


def _sample_pre(x, npre, win, inv, convw, convb, wg, br, bi, lam, h0, sc, *, past_len):
    rows = x.shape[0]
    out_shape = tuple(jax.ShapeDtypeStruct((rows, w), F32)
                      for w in (D_ATT, D_ATT, D_ATT, D_ATT, D_LRU, D_LRU, (CONV_WIDTH - 1) * D_LRU))
    return pl.pallas_call(
        functools.partial(_sample_pre_kernel, past_len=past_len),
        out_shape=out_shape,
        compiler_params=pltpu.CompilerParams(vmem_limit_bytes=VMEM_LIMIT),
        name="sample_pre",
    )(x, npre, win, inv, convw, convb, wg, br, bi, lam, h0, sc)


def _tile(n, target):
    t = min(n, target)
    assert n % t == 0, (n, t)
    return t


def kernel(x_prompt, x_sample, cache_k, cache_v, state_h, state_conv, page_table, meta_tokens, norm_pre, w_in, conv_w, conv_b, w_gate_r, b_gate_r, w_gate_i, b_gate_i, lru_lambda, lambda_q1, lambda_k1, lambda_q2, lambda_k2, attn_subnorm, w_out, norm_post):
    assert x_prompt.shape[0] == 1 and x_sample.shape[1] == 1 and w_in.shape[0] == 1
    seq, d_model = x_prompt.shape[1], x_prompt.shape[2]
    db = x_sample.shape[0]
    n_pool, page = cache_k.shape[1], cache_k.shape[2]
    past_len = page_table.shape[1] * page

    win = w_in[0].astype(BF16)
    wout = w_out[0].astype(BF16)
    row = lambda a: a.reshape(1, -1)
    npre, npost, sub = row(norm_pre[0]), row(norm_post[0]), row(attn_subnorm[0])
    convw, convb = conv_w[0], row(conv_b[0])
    br, bi, lam = row(b_gate_r[0]), row(b_gate_i[0]), row(lru_lambda[0])
    blocks = jnp.arange(D_LRU) // (D_LRU // N_LRU_BLOCKS)
    same_block = blocks[:, None] == blocks[None, :]
    bdiag = lambda w: jnp.where(same_block, jnp.tile(w.reshape(D_LRU, -1), (1, N_LRU_BLOCKS)), 0.0)
    wg = jnp.concatenate([bdiag(w_gate_r[0]), bdiag(w_gate_i[0])], axis=1).astype(BF16)
    lq1, lk1, lq2, lk2 = row(lambda_q1[0]), row(lambda_k1[0]), row(lambda_q2[0]), row(lambda_k2[0])
    inv = ROPE_THETA ** (-jnp.arange(0, HEAD_DIM, 2, dtype=F32) / HEAD_DIM)
    inv = jnp.tile(inv, LANES // (HEAD_DIM // 2)).reshape(1, LANES)

    xp, xs = x_prompt[0], x_sample[:, 0]
    tr = _tile(seq, 512)
    (q, katt, vt, gatt, ylg, kmatt, vtm, hlast, convlast, k_rows, v_rows) = _prompt_pre(
        xp, meta_tokens, npre, win, inv, convw, convb, wg, br, bi, lam, tr=tr)
    qs, ks, vs, gatt_s, ylg_s, h_s, conv_s = _sample_pre(
        xs, npre, win, inv, convw, convb, wg, br, bi, lam, state_h[0],
        state_conv[0].reshape(db, (CONV_WIDTH - 1) * D_LRU), past_len=past_len)

    per_head = lambda a: a.reshape(db, N_ATT_HEADS, HEAD_V)
    o_att, o_s = _attention(page_table, lq1, lk1, lq2, lk2, q, katt, vt, kmatt, vtm,
                            per_head(qs), per_head(ks), per_head(vs),
                            cache_k[0].reshape(n_pool, page * N_ATT_HEADS, HEAD_V),
                            cache_v[0].reshape(n_pool, page * N_ATT_HEADS, HEAD_V),
                            tq=_tile(seq, 512), tk=_tile(seq, 1024))

    y_prompt = _epilogue(xp, o_att, gatt, ylg, sub, wout, npost, tr=tr)
    y_sample = _epilogue(xs, o_s.reshape(db, D_ATT), gatt_s, ylg_s, sub, wout, npost, tr=db)

    t_all = seq + N_META
    k_prompt = k_rows.reshape(1, 1, t_all, N_ATT_HEADS, HEAD_V)
    v_prompt = v_rows.reshape(1, 1, t_all, N_ATT_HEADS, HEAD_V)
    return (y_prompt[None], y_sample[:, None], k_prompt, v_prompt,
            hlast[0:1][None], convlast[SUBLANES - (CONV_WIDTH - 1):][None, None],
            ks.reshape(1, db, 1, N_ATT_HEADS, HEAD_V), vs.reshape(1, db, 1, N_ATT_HEADS, HEAD_V),
            h_s[None], conv_s.reshape(1, db, CONV_WIDTH - 1, D_LRU))
```

```python
import functools
import math

import jax
import jax.numpy as jnp
from jax import lax
from jax.experimental import pallas as pl
from jax.experimental.pallas import tpu as pltpu

N_META = 16
N_ATT_HEADS = 4
HEAD_DIM = 64
HEAD_V = 2 * HEAD_DIM
D_ATT = N_ATT_HEADS * HEAD_V
D_LRU = 512
N_LRU_BLOCKS = 8
CONV_WIDTH = 4
LRU_C = 8.0
ROPE_THETA = 10000.0
EPS = 1e-6
LAM_INIT = 0.8 - 0.6 * math.exp(-0.3 * 0)
Q_SCALE = HEAD_DIM ** -0.5 * math.log2(math.e)

LANES = 128
SUBLANES = 8
VMEM_LIMIT = 56 * 1024 * 1024
BF16_ROWS = 16

F32 = jnp.float32
BF16 = jnp.bfloat16


def _rms(x, g):
    return x * lax.rsqrt(jnp.mean(x * x, axis=-1, keepdims=True) + EPS) * g


def _silu(x):
    return x * jax.nn.sigmoid(x)


def _first_half_mask(shape):
    lane = lax.broadcasted_iota(jnp.int32, shape, len(shape) - 1)
    return (lane % HEAD_DIM) < (HEAD_DIM // 2)


def _rope_block(zb, cos, sin_signed, first_half):
    half = HEAD_DIM // 2
    partner = jnp.where(first_half, pltpu.roll(zb, LANES - half, axis=1), pltpu.roll(zb, half, axis=1))
    return zb * cos + partner * sin_signed


def _diff_lambda(lq1, lk1, lq2, lk2):
    return (jnp.exp(jnp.sum(lq1[...] * lk1[...])) - jnp.exp(jnp.sum(lq2[...] * lk2[...])) + LAM_INIT)


def _lru_gates(xc, wg_ref, br_ref, bi_ref, lam_ref):
    gates = jnp.dot(xc.astype(BF16), wg_ref[...], preferred_element_type=F32)
    r = jax.nn.sigmoid(gates[:, :D_LRU] + br_ref[...])
    i = jax.nn.sigmoid(gates[:, D_LRU:] + bi_ref[...])
    neg_lam = -lam_ref[...]
    softplus = jnp.maximum(neg_lam, 0.0) + jnp.log1p(jnp.exp(-jnp.abs(neg_lam)))
    log_a = -LRU_C * r * softplus
    a = jnp.exp(log_a)
    one_minus_a2 = 1.0 - jnp.exp(2.0 * log_a)
    root = jnp.where(one_minus_a2 > 0.0, one_minus_a2 * lax.rsqrt(one_minus_a2), 0.0)
    u = root * (i * xc)
    return a, u


def _prompt_pre_kernel(x_ref, meta_ref, npre_ref, win_ref, inv_ref, convw_ref, convb_ref, wg_ref,
                       br_ref, bi_ref, lam_ref,
                       q_ref, katt_ref, vt_ref, gatt_ref, ylg_ref,
                       kmatt_ref, vtm_ref, hlast_ref, convlast_ref, k_hbm, v_hbm,
                       cos_tab, sin_tab, conv_carry, h_carry, kbuf, vbuf, kmbuf, vmbuf, sems, msems, *, tr):
    step = pl.program_id(0)
    nh = N_ATT_HEADS

    def row_copies(slot, tile):
        rows = pl.ds((N_META + tile * tr) * nh, tr * nh)
        return (pltpu.make_async_copy(kbuf.at[slot], k_hbm.at[rows], sems.at[0, slot]),
                pltpu.make_async_copy(vbuf.at[slot], v_hbm.at[rows], sems.at[1, slot]))

    def meta_copies():
        rows = pl.ds(0, N_META * nh)
        return (pltpu.make_async_copy(kmbuf, k_hbm.at[rows], msems.at[0]),
                pltpu.make_async_copy(vmbuf, v_hbm.at[rows], msems.at[1]))

    def stage_rows(kdst, vdst, ks, zv, rows):
        for h in range(nh):
            kdst[pl.ds(h, rows, stride=nh), :] = ks[h][0:rows]
            vdst[pl.ds(h, rows, stride=nh), :] = zv[0:rows, h * HEAD_V:(h + 1) * HEAD_V]

    def rope_tables(pos0, rows):
        base = pos0.astype(F32) * inv_ref[...]
        cb, sb = jnp.cos(base), jnp.sin(base)
        cr, sr = cos_tab[0:rows, :], sin_tab[0:rows, :]
        cos = cr * cb - sr * sb
        sin = sr * cb + cr * sb
        return cos, jnp.where(_first_half_mask((rows, LANES)), -sin, sin)

    def project(x, pos0, rows):
        u = _rms(x, npre_ref[...]).astype(BF16)
        cos, sin_signed = rope_tables(pos0, rows)
        fh = _first_half_mask((rows, LANES))
        zq = jnp.dot(u, win_ref[:, 0:D_ATT], preferred_element_type=F32)
        zk = jnp.dot(u, win_ref[:, D_ATT:2 * D_ATT], preferred_element_type=F32)
        zv = jnp.dot(u, win_ref[:, 2 * D_ATT:3 * D_ATT], preferred_element_type=F32)
        qs, ks = [], []
        for h in range(N_ATT_HEADS):
            sl = slice(h * HEAD_V, (h + 1) * HEAD_V)
            qs.append(_rope_block(zq[:, sl], cos, sin_signed, fh) * Q_SCALE)
            ks.append(_rope_block(zk[:, sl], cos, sin_signed, fh))
        return u, qs, ks, zv

    def lru(xb, glru, rows, store_y):
        xp = jnp.concatenate([conv_carry[...], xb], axis=0)
        off = SUBLANES - (CONV_WIDTH - 1)
        xc = convb_ref[...] + convw_ref[0:1, :] * xp[off:off + rows]
        for j in range(1, CONV_WIDTH):
            xc = xc + convw_ref[j:j + 1, :] * xp[off + j:off + j + rows]
        conv_carry[...] = xp[rows:rows + SUBLANES]
        a, u = _lru_gates(xc, wg_ref, br_ref, bi_ref, lam_ref)
        row = lax.broadcasted_iota(jnp.int32, (rows, D_LRU), 0) % SUBLANES
        d = 1
        while d < SUBLANES:
            valid = row >= d
            u = jnp.where(valid, a * pltpu.roll(u, d, axis=0) + u, u)
            a = jnp.where(valid, a * pltpu.roll(a, d, axis=0), a)
            d *= 2
        h = h_carry[...]
        pending = []
        for g in range(rows // SUBLANES):
            sl = slice(g * SUBLANES, (g + 1) * SUBLANES)
            hg = a[sl] * h + u[sl]
            if store_y:
                pending.append(hg * _silu(glru[sl]))
                if len(pending) == BF16_ROWS // SUBLANES:
                    lo = (g + 1) * SUBLANES - BF16_ROWS
                    ylg_ref[lo:lo + BF16_ROWS, :] = jnp.concatenate(pending, axis=0).astype(BF16)
                    pending = []
            h = jnp.broadcast_to(hg[SUBLANES - 1:SUBLANES, :], (SUBLANES, D_LRU))
        assert not pending
        h_carry[...] = h

    @pl.when(step == 0)
    def _():
        r = lax.broadcasted_iota(jnp.int32, (tr, LANES), 0).astype(F32)
        ang = r * inv_ref[...]
        cos_tab[...] = jnp.cos(ang)
        sin_tab[...] = jnp.sin(ang)
        conv_carry[...] = jnp.zeros_like(conv_carry)
        h_carry[...] = jnp.zeros_like(h_carry)
        xm = jnp.concatenate([meta_ref[...], jnp.zeros((LANES - N_META, meta_ref.shape[1]), F32)], axis=0)
        u, _, ks, zv = project(xm, jnp.zeros((), jnp.int32), LANES)
        for h in range(N_ATT_HEADS):
            sl = slice(h * HEAD_V, (h + 1) * HEAD_V)
            kmatt_ref[h] = ks[h].astype(BF16)
            vtm_ref[h] = zv[:, sl].T.astype(BF16)
        stage_rows(kmbuf, vmbuf, ks, zv, N_META)
        for cp in meta_copies():
            cp.start()
        um = u[0:N_META]
        xb = jnp.dot(um, win_ref[:, 4 * D_ATT:4 * D_ATT + D_LRU], preferred_element_type=F32)
        lru(xb, None, N_META, False)

    u, qs, ks, zv = project(x_ref[...], N_META + step * tr, tr)
    for h in range(N_ATT_HEADS):
        sl = slice(h * HEAD_V, (h + 1) * HEAD_V)
        q_ref[h] = qs[h].astype(BF16)
        katt_ref[h] = ks[h].astype(BF16)
        vt_ref[h] = zv[:, sl].T.astype(BF16)
    slot = step % 2

    @pl.when(step >= 2)
    def _():
        for cp in row_copies(slot, step - 2):
            cp.wait()

    stage_rows(kbuf.at[slot], vbuf.at[slot], ks, zv, tr)
    for cp in row_copies(slot, step):
        cp.start()
    gatt_ref[...] = jnp.dot(u, win_ref[:, 3 * D_ATT:4 * D_ATT], preferred_element_type=F32)
    xb = jnp.dot(u, win_ref[:, 4 * D_ATT:4 * D_ATT + D_LRU], preferred_element_type=F32)
    glru = jnp.dot(u, win_ref[:, 4 * D_ATT + D_LRU:], preferred_element_type=F32)
    lru(xb, glru, tr, True)
    hlast_ref[...] = h_carry[...]
    convlast_ref[...] = conv_carry[...]

    @pl.when(step == pl.num_programs(0) - 1)
    def _():
        for cp in meta_copies():
            cp.wait()
        for cp in row_copies(slot, step):
            cp.wait()

        @pl.when(step >= 1)
        def _():
            for cp in row_copies(1 - slot, step - 1):
                cp.wait()


def _prompt_pre(x, meta, npre, win, inv, convw, convb, wg, br, bi, lam, *, tr):
    s, d = x.shape
    nt = s // tr
    full = lambda a: pl.BlockSpec(a.shape, lambda i: (0,) * a.ndim)
    out_shape = (
        jax.ShapeDtypeStruct((N_ATT_HEADS, s, HEAD_V), BF16),
        jax.ShapeDtypeStruct((N_ATT_HEADS, s, HEAD_V), BF16),
        jax.ShapeDtypeStruct((N_ATT_HEADS, HEAD_V, s), BF16),
        jax.ShapeDtypeStruct((s, D_ATT), F32),
        jax.ShapeDtypeStruct((s, D_LRU), BF16),
        jax.ShapeDtypeStruct((N_ATT_HEADS, LANES, HEAD_V), BF16),
        jax.ShapeDtypeStruct((N_ATT_HEADS, HEAD_V, LANES), BF16),
        jax.ShapeDtypeStruct((SUBLANES, D_LRU), F32),
        jax.ShapeDtypeStruct((SUBLANES, D_LRU), F32),
        jax.ShapeDtypeStruct(((s + N_META) * N_ATT_HEADS, HEAD_V), F32),
        jax.ShapeDtypeStruct(((s + N_META) * N_ATT_HEADS, HEAD_V), F32),
    )
    out_specs = (
        pl.BlockSpec((N_ATT_HEADS, tr, HEAD_V), lambda i: (0, i, 0)),
        pl.BlockSpec((N_ATT_HEADS, tr, HEAD_V), lambda i: (0, i, 0)),
        pl.BlockSpec((N_ATT_HEADS, HEAD_V, tr), lambda i: (0, 0, i)),
        pl.BlockSpec((tr, D_ATT), lambda i: (i, 0)),
        pl.BlockSpec((tr, D_LRU), lambda i: (i, 0)),
    ) + tuple(pl.BlockSpec(o.shape, lambda i, n=len(o.shape): (0,) * n) for o in out_shape[5:9]) + (
        pl.BlockSpec(memory_space=pl.ANY), pl.BlockSpec(memory_space=pl.ANY))
    return pl.pallas_call(
        functools.partial(_prompt_pre_kernel, tr=tr),
        grid=(nt,),
        in_specs=[pl.BlockSpec((tr, d), lambda i: (i, 0))] + [full(a) for a in (meta, npre, win, inv, convw, convb, wg, br, bi, lam)],
        out_specs=out_specs,
        out_shape=out_shape,
        scratch_shapes=[pltpu.VMEM((tr, LANES), F32), pltpu.VMEM((tr, LANES), F32),
                        pltpu.VMEM((SUBLANES, D_LRU), F32), pltpu.VMEM((SUBLANES, D_LRU), F32),
                        pltpu.VMEM((2, tr * N_ATT_HEADS, HEAD_V), F32), pltpu.VMEM((2, tr * N_ATT_HEADS, HEAD_V), F32),
                        pltpu.VMEM((N_META * N_ATT_HEADS, HEAD_V), F32), pltpu.VMEM((N_META * N_ATT_HEADS, HEAD_V), F32),
                        pltpu.SemaphoreType.DMA((2, 2)), pltpu.SemaphoreType.DMA((2,))],
        compiler_params=pltpu.CompilerParams(dimension_semantics=("arbitrary",), vmem_limit_bytes=VMEM_LIMIT),
        name="prompt_pre",
    )(x, meta, npre, win, inv, convw, convb, wg, br, bi, lam)


DECODE_PAGES_PER_REGION = 8
DECODE_RING = 32


def _n_chunks(i, tq, tk):
    return (i * tq) // tk + 1


def _attn_kernel(pt_ref, lq1, lk1, lq2, lk2, q_ref, qn_ref, k_ref, vt_ref, kmeta_ref, vtmeta_ref,
                 qs_ref, kn_ref, vn_ref, ck_hbm, cv_hbm, o_ref, os_ref,
                 m_sc, l_sc, acc_sc, sa_sc, sb_sc, cma_sc, cmb_sc, kring, vring, sems, g_sc, dm_sc, dl_sc, dacc_sc,
                 *, tq, tk, n_pages, total_pages):
    h, i = pl.program_id(0), pl.program_id(1)
    pp, ring, nh = DECODE_PAGES_PER_REGION, DECODE_RING, N_ATT_HEADS
    n_unmasked = _n_chunks(i, tq, tk) - 1
    lam = _diff_lambda(lq1, lk1, lq2, lk2)

    def page_copies(flat, slot):
        page = pt_ref[jnp.minimum(flat, total_pages - 1)]
        return (pltpu.make_async_copy(ck_hbm.at[page], kring.at[slot], sems.at[0, slot]),
                pltpu.make_async_copy(cv_hbm.at[page], vring.at[slot], sems.at[1, slot]))

    @pl.when((h == 0) & (i == 0))
    def _():
        g_sc[0] = 0
        dm_sc[...] = jnp.zeros_like(dm_sc)
        dl_sc[...] = jnp.zeros_like(dl_sc)
        dacc_sc[...] = jnp.zeros_like(dacc_sc)
        for t in range(ring):
            for cp in page_copies(t, t):
                cp.start()

    def decode_wait(g, n):
        for t in range(n):
            for cp in page_copies(g + t, (g + t) % ring):
                cp.wait()

    def decode_start(g, n):
        for t in range(n):
            for cp in page_copies(g + ring + t, (g + t) % ring):
                cp.start()
        g_sc[0] = g + n

    def decode_begin(g):
        gc = jnp.minimum(g, total_pages - pp)
        d = dict(active=g < total_pages,
                 b=gc // n_pages, first=((gc % n_pages) == 0) & (g < total_pages), slot0=g % ring, s=[], pv=None)
        q4 = qs_ref[d["b"]]
        lane = lax.broadcasted_iota(jnp.int32, q4.shape, 1)
        q1, q2 = jnp.where(lane < HEAD_DIM, q4, 0.0), jnp.where(lane >= HEAD_DIM, q4, 0.0)
        d["qrows"] = jnp.concatenate([q1, q2], axis=0).astype(BF16)
        kn = kn_ref[d["b"]]
        d["m0"] = jnp.concatenate([jnp.sum(q1 * kn, axis=1, keepdims=True),
                                   jnp.sum(q2 * kn, axis=1, keepdims=True)], axis=0)
        return d

    def decode_scores(d, t):
        kpage = kring[d["slot0"] + t].astype(BF16)
        d["s"].append(lax.dot_general(d["qrows"], kpage, (((1,), (1,)), ((), ())),
                                      preferred_element_type=F32))

    def decode_softmax(d):
        s, first = jnp.concatenate(d["s"], axis=1), d["first"]
        head_of_col = lax.broadcasted_iota(jnp.int32, s.shape, 1) % nh
        head_of_row = lax.broadcasted_iota(jnp.int32, s.shape, 0) % nh
        s = jnp.where(head_of_col == head_of_row, s, -jnp.inf)
        d["m_old"] = jnp.where(first, d["m0"], dm_sc[:, 0:1])
        d["l_old"] = jnp.where(first, 1.0, dl_sc[:, 0:1])
        d["m_new"] = jnp.maximum(d["m_old"], jnp.max(s, axis=1, keepdims=True))
        d["alpha"] = jnp.exp2(d["m_old"] - d["m_new"])
        p = jnp.exp2(s - d["m_new"])
        d["l_new"] = d["alpha"] * d["l_old"] + jnp.sum(p, axis=1, keepdims=True)
        d["p"] = p.astype(BF16)

    def decode_values(d, t):
        prow = kring.shape[1]
        pv = jnp.dot(d["p"][:, t * prow:(t + 1) * prow], vring[d["slot0"] + t].astype(BF16),
                     preferred_element_type=F32)
        d["pv"] = pv if d["pv"] is None else d["pv"] + pv

    def decode_finish(d):
        active, first, b = d["active"], d["first"], d["b"]
        vn = vn_ref[b]
        acc_old = jnp.where(first, jnp.concatenate([vn, vn], axis=0), dacc_sc[...])
        acc_new = d["alpha"] * acc_old + d["pv"]
        m_keep = jnp.where(active, d["m_new"], d["m_old"])
        l_keep = jnp.where(active, d["l_new"], d["l_old"])
        acc_keep = jnp.where(active, acc_new, acc_old)
        dm_sc[...] = jnp.broadcast_to(m_keep, dm_sc.shape)
        dl_sc[...] = jnp.broadcast_to(l_keep, dl_sc.shape)
        dacc_sc[...] = acc_keep
        o = acc_keep / l_keep
        os_ref[b] = o[0:nh] - lam * o[nh:2 * nh]

    def branch_columns(q_tile):
        qt = q_tile.astype(F32).T
        rows = lax.broadcasted_iota(jnp.int32, qt.shape, 0)
        return jnp.concatenate([jnp.where(rows < HEAD_DIM, qt, 0.0), jnp.where(rows >= HEAD_DIM, qt, 0.0)],
                               axis=1).astype(BF16)

    qb = branch_columns(q_ref[0])
    first_ready = (i >= 1) & ((_n_chunks(i - 1, tq, tk) - 1) % 2 == 1)

    cb = 2 * LANES
    n_cb = 2 * tq // cb
    half = n_cb // 2
    assert n_cb % 2 == 0 and pp % half == 0
    ppb = pp // half

    def score_slab(c, j, qcols=None):
        start = pl.multiple_of(c * tk, tk)
        qcols = qb if qcols is None else qcols
        return jnp.dot(k_ref[0, pl.ds(start, tk), :], qcols[:, j * cb:(j + 1) * cb],
                       preferred_element_type=F32)

    def causal_slab(j, rows):
        kpos = n_unmasked * tk + lax.broadcasted_iota(jnp.int32, (rows, cb), 0)
        qpos = i * tq + (j * cb + lax.broadcasted_iota(jnp.int32, (rows, cb), 1)) % tq
        return kpos <= qpos

    def region(cur, c_cur, causal, nxt, c_nxt, g, rows=None, nxt_qcols=None):
        cur_ref, cur_cm = cur
        rows = tk if rows is None else rows
        vtt = vt_ref[0, :, pl.ds(pl.multiple_of(c_cur * tk, tk), rows)]
        d = decode_begin(g)
        for j in range(n_cb):
            sl = slice(j * cb, (j + 1) * cb)
            if j < half:
                for u in range(ppb):
                    decode_scores(d, j * ppb + u)
            if nxt is not None:
                s_nxt = score_slab(c_nxt, j, nxt_qcols)
                nxt[0][:, sl] = s_nxt
                nxt[1][:, sl] = jnp.max(s_nxt, axis=0, keepdims=True)
            if j == half:
                decode_softmax(d)
            s_cur = cur_ref[0:rows, sl]
            if causal:
                s_cur = jnp.where(causal_slab(j, rows), s_cur, -jnp.inf)
                cmax = jnp.max(s_cur, axis=0, keepdims=True)
            else:
                cmax = cur_cm[:, sl]
            m_old = m_sc[:, sl]
            m_new = jnp.maximum(m_old, cmax)
            alpha = jnp.exp2(m_old - m_new)
            p = jnp.exp2(s_cur - m_new)
            l_sc[:, sl] = alpha * l_sc[:, sl] + jnp.sum(p, axis=0, keepdims=True)
            acc_sc[:, sl] = alpha * acc_sc[:, sl] + jnp.dot(vtt, p.astype(BF16), preferred_element_type=F32)
            m_sc[:, sl] = m_new
            if j >= half:
                for u in range(ppb):
                    decode_values(d, (j - half) * ppb + u)
        decode_finish(d)

    krow = lax.broadcasted_iota(jnp.int32, (LANES, 2 * tq), 0)
    s_meta = jnp.where(krow < N_META, jnp.dot(kmeta_ref[0], qb, preferred_element_type=F32), -jnp.inf)
    m_meta = jnp.max(s_meta, axis=0, keepdims=True)
    p_meta = jnp.exp2(s_meta - m_meta)
    m_sc[...] = m_meta
    l_sc[...] = jnp.sum(p_meta, axis=0, keepdims=True)
    acc_sc[...] = jnp.dot(vtmeta_ref[0], p_meta.astype(BF16), preferred_element_type=F32)

    buf_a, buf_b = (sa_sc, cma_sc), (sb_sc, cmb_sc)

    @pl.when(jnp.logical_not(first_ready))
    def _():
        for j in range(n_cb):
            s0 = score_slab(0, j)
            sa_sc[:, j * cb:(j + 1) * cb] = s0
            cma_sc[:, j * cb:(j + 1) * cb] = jnp.max(s0, axis=0, keepdims=True)

    def pair(t, carry):
        c, g = 2 * t, g_sc[0]
        decode_wait(g, 2 * pp)
        region(buf_a, c, False, buf_b, c + 1, g)
        region(buf_b, c + 1, False, buf_a, c + 2, g + pp)
        decode_start(g, 2 * pp)
        return carry

    lax.fori_loop(0, n_unmasked // 2, pair, 0)

    tiles_per_chunk = tk // tq
    for phase in range(tiles_per_chunk):
        live = (phase + 1) * tq

        @pl.when((n_unmasked % 2 == 1) & (i % tiles_per_chunk == phase))
        def _():
            g = g_sc[0]
            decode_wait(g, 2 * pp)
            region(buf_a, n_unmasked - 1, False, buf_b, n_unmasked, g)
            region(buf_b, n_unmasked, True, buf_a, 0, g + pp, rows=live, nxt_qcols=branch_columns(qn_ref[0]))
            decode_start(g, 2 * pp)

        @pl.when((n_unmasked % 2 == 0) & (i % tiles_per_chunk == phase))
        def _():
            g = g_sc[0]
            decode_wait(g, pp)
            region(buf_a, n_unmasked, True, None, None, g, rows=live)
            decode_start(g, pp)

    o = acc_sc[...] * (1.0 / l_sc[...])
    o_ref[...] = (o[:, :tq] - lam * o[:, tq:]).T

    @pl.when((h == pl.num_programs(0) - 1) & (i == pl.num_programs(1) - 1))
    def _():
        g = g_sc[0]
        for t in range(ring):
            for cp in page_copies(g + t, (g + t) % ring):
                cp.wait()


def _attention(page_table, lq1, lk1, lq2, lk2, q, katt, vt, kmatt, vtm, qs, kn, vn, cache_k, cache_v, *, tq, tk):
    nh, s, _ = q.shape
    db, n_pages = page_table.shape
    _, prow, width = cache_k.shape
    total_pages = db * n_pages
    n_slices = nh * sum(_n_chunks(i, tq, tk) for i in range(s // tq))
    assert n_slices * DECODE_PAGES_PER_REGION >= total_pages, "prompt too short to carry the decode stream"
    assert n_pages % DECODE_PAGES_PER_REGION == 0 and DECODE_RING >= 4 * DECODE_PAGES_PER_REGION
    small = lambda a: pl.BlockSpec(a.shape, lambda h, i, pt: (0,) * a.ndim)
    hbm = pl.BlockSpec(memory_space=pl.ANY)
    grid_spec = pltpu.PrefetchScalarGridSpec(
        num_scalar_prefetch=1,
        grid=(nh, s // tq),
        in_specs=[small(lq1), small(lk1), small(lq2), small(lk2),
                  pl.BlockSpec((1, tq, HEAD_V), lambda h, i, pt: (h, i, 0)),
                  pl.BlockSpec((1, tq, HEAD_V), lambda h, i, pt: (h, jnp.minimum(i + 1, s // tq - 1), 0)),
                  pl.BlockSpec((1, s, HEAD_V), lambda h, i, pt: (h, 0, 0)),
                  pl.BlockSpec((1, HEAD_V, s), lambda h, i, pt: (h, 0, 0)),
                  pl.BlockSpec((1, LANES, HEAD_V), lambda h, i, pt: (h, 0, 0)),
                  pl.BlockSpec((1, HEAD_V, LANES), lambda h, i, pt: (h, 0, 0)),
                  small(qs), small(kn), small(vn), hbm, hbm],
        out_specs=[pl.BlockSpec((tq, HEAD_V), lambda h, i, pt: (i, h)),
                   pl.BlockSpec((db, N_ATT_HEADS, width), lambda h, i, pt: (0, 0, 0))],
        scratch_shapes=[pltpu.VMEM((1, 2 * tq), F32), pltpu.VMEM((1, 2 * tq), F32),
                        pltpu.VMEM((HEAD_V, 2 * tq), F32),
                        pltpu.VMEM((tk, 2 * tq), F32), pltpu.VMEM((tk, 2 * tq), F32),
                        pltpu.VMEM((1, 2 * tq), F32), pltpu.VMEM((1, 2 * tq), F32),
                        pltpu.VMEM((DECODE_RING, prow, width), F32), pltpu.VMEM((DECODE_RING, prow, width), F32),
                        pltpu.SemaphoreType.DMA((2, DECODE_RING)), pltpu.SMEM((1,), jnp.int32),
                        pltpu.VMEM((2 * N_ATT_HEADS, LANES), F32), pltpu.VMEM((2 * N_ATT_HEADS, LANES), F32),
                        pltpu.VMEM((2 * N_ATT_HEADS, width), F32)],
    )
    return pl.pallas_call(
        functools.partial(_attn_kernel, tq=tq, tk=tk, n_pages=n_pages, total_pages=total_pages),
        grid_spec=grid_spec,
        out_shape=(jax.ShapeDtypeStruct((s, D_ATT), F32), jax.ShapeDtypeStruct((db, N_ATT_HEADS, width), F32)),
        compiler_params=pltpu.CompilerParams(dimension_semantics=("arbitrary", "arbitrary"),
                                             vmem_limit_bytes=VMEM_LIMIT),
        name="attention",
    )(page_table.reshape(-1), lq1, lk1, lq2, lk2, q, q, katt, vt, kmatt, vtm, qs, kn, vn, cache_k, cache_v)


def _epilogue_kernel(x_ref, o_ref, gatt_ref, ylg_ref, sub_ref, wout_ref, npost_ref, y_ref):
    parts = []
    for h in range(N_ATT_HEADS):
        sl = slice(h * HEAD_V, (h + 1) * HEAD_V)
        o = _rms(o_ref[:, sl], sub_ref[...]) * (1.0 - LAM_INIT)
        parts.append((o * _silu(gatt_ref[:, sl])).astype(BF16))
    parts.append(ylg_ref[...].astype(BF16))
    mix = jnp.dot(jnp.concatenate(parts, axis=1), wout_ref[...], preferred_element_type=F32)
    y_ref[...] = x_ref[...] + _rms(mix, npost_ref[...])


def _epilogue(x, o_att, gatt, ylg, sub, wout, npost, *, tr):
    s, d = x.shape
    full = lambda a: pl.BlockSpec(a.shape, lambda i: (0,) * a.ndim)
    rows = lambda w: pl.BlockSpec((tr, w), lambda i: (i, 0))
    return pl.pallas_call(
        _epilogue_kernel,
        grid=(s // tr,),
        in_specs=[rows(d), rows(D_ATT), rows(D_ATT), rows(D_LRU), full(sub), full(wout), full(npost)],
        out_specs=rows(d),
        out_shape=jax.ShapeDtypeStruct((s, d), F32),
        compiler_params=pltpu.CompilerParams(dimension_semantics=("arbitrary",), vmem_limit_bytes=VMEM_LIMIT),
        name="epilogue",
    )(x, o_att, gatt, ylg, sub, wout, npost)


def _sample_pre_kernel(x_ref, npre_ref, win_ref, inv_ref, convw_ref, convb_ref, wg_ref, br_ref, bi_ref,
                       lam_ref, h0_ref, sc_ref,
                       q_ref, k_ref, v_ref, gatt_ref, ylg_ref, h_ref, cnew_ref, *, past_len):
    rows = x_ref.shape[0]
    u = _rms(x_ref[...], npre_ref[...]).astype(BF16)
    z = jnp.dot(u, win_ref[...], preferred_element_type=F32)
    ang = float(past_len) * inv_ref[...]
    cos, sin = jnp.cos(ang), jnp.sin(ang)
    fh = _first_half_mask((rows, LANES))
    sin_signed = jnp.where(fh[0:1], -sin, sin)
    for h in range(N_ATT_HEADS):
        sl = slice(h * HEAD_V, (h + 1) * HEAD_V)
        q_ref[:, sl] = _rope_block(z[:, sl], cos, sin_signed, fh) * Q_SCALE
        k_ref[:, sl] = _rope_block(z[:, D_ATT + h * HEAD_V:D_ATT + (h + 1) * HEAD_V], cos, sin_signed, fh)
    v_ref[...] = z[:, 2 * D_ATT:3 * D_ATT]
    gatt_ref[...] = z[:, 3 * D_ATT:4 * D_ATT]
    xb = z[:, 4 * D_ATT:4 * D_ATT + D_LRU]
    glru = z[:, 4 * D_ATT + D_LRU:]
    xc = convb_ref[...] + convw_ref[CONV_WIDTH - 1:CONV_WIDTH, :] * xb
    for j in range(CONV_WIDTH - 1):
        xc = xc + convw_ref[j:j + 1, :] * sc_ref[:, j * D_LRU:(j + 1) * D_LRU]
    a, uu = _lru_gates(xc, wg_ref, br_ref, bi_ref, lam_ref)
    hn = a * h0_ref[...] + uu
    h_ref[...] = hn
    ylg_ref[...] = hn * _silu(glru)
    cnew_ref[:, 0:(CONV_WIDTH - 2) * D_LRU] = sc_ref[:, D_LRU:]
    cnew_ref[:, (CONV_WIDTH - 2) * D_LRU:] = xb


def _sample_pre(x, npre, win, inv, convw, convb, wg, br, bi, lam, h0, sc, *, past_len):
    rows = x.shape[0]
    out_shape = tuple(jax.ShapeDtypeStruct((rows, w), F32)
                      for w in (D_ATT, D_ATT, D_ATT, D_ATT, D_LRU, D_LRU, (CONV_WIDTH - 1) * D_LRU))
    return pl.pallas_call(
        functools.partial(_sample_pre_kernel, past_len=past_len),
        out_shape=out_shape,
        compiler_params=pltpu.CompilerParams(vmem_limit_bytes=VMEM_LIMIT),
        name="sample_pre",
    )(x, npre, win, inv, convw, convb, wg, br, bi, lam, h0, sc)


def _tile(n, target):
    t = min(n, target)
    assert n % t == 0, (n, t)
    return t


def kernel(x_prompt, x_sample, cache_k, cache_v, state_h, state_conv, page_table, meta_tokens, norm_pre, w_in, conv_w, conv_b, w_gate_r, b_gate_r, w_gate_i, b_gate_i, lru_lambda, lambda_q1, lambda_k1, lambda_q2, lambda_k2, attn_subnorm, w_out, norm_post):
    assert x_prompt.shape[0] == 1 and x_sample.shape[1] == 1 and w_in.shape[0] == 1
    seq, d_model = x_prompt.shape[1], x_prompt.shape[2]
    db = x_sample.shape[0]
    n_pool, page = cache_k.shape[1], cache_k.shape[2]
    past_len = page_table.shape[1] * page

    win = w_in[0].astype(BF16)
    wout = w_out[0].astype(BF16)
    row = lambda a: a.reshape(1, -1)
    npre, npost, sub = row(norm_pre[0]), row(norm_post[0]), row(attn_subnorm[0])
    convw, convb = conv_w[0], row(conv_b[0])
    br, bi, lam = row(b_gate_r[0]), row(b_gate_i[0]), row(lru_lambda[0])
    blocks = jnp.arange(D_LRU) // (D_LRU // N_LRU_BLOCKS)
    same_block = blocks[:, None] == blocks[None, :]
    bdiag = lambda w: jnp.where(same_block, jnp.tile(w.reshape(D_LRU, -1), (1, N_LRU_BLOCKS)), 0.0)
    wg = jnp.concatenate([bdiag(w_gate_r[0]), bdiag(w_gate_i[0])], axis=1).astype(BF16)
    lq1, lk1, lq2, lk2 = row(lambda_q1[0]), row(lambda_k1[0]), row(lambda_q2[0]), row(lambda_k2[0])
    inv = ROPE_THETA ** (-jnp.arange(0, HEAD_DIM, 2, dtype=F32) / HEAD_DIM)
    inv = jnp.tile(inv, LANES // (HEAD_DIM // 2)).reshape(1, LANES)

    xp, xs = x_prompt[0], x_sample[:, 0]
    tr = _tile(seq, 512)
    (q, katt, vt, gatt, ylg, kmatt, vtm, hlast, convlast, k_rows, v_rows) = _prompt_pre(
        xp, meta_tokens, npre, win, inv, convw, convb, wg, br, bi, lam, tr=tr)
    qs, ks, vs, gatt_s, ylg_s, h_s, conv_s = _sample_pre(
        xs, npre, win, inv, convw, convb, wg, br, bi, lam, state_h[0],
        state_conv[0].reshape(db, (CONV_WIDTH - 1) * D_LRU), past_len=past_len)

    per_head = lambda a: a.reshape(db, N_ATT_HEADS, HEAD_V)
    o_att, o_s = _attention(page_table, lq1, lk1, lq2, lk2, q, katt, vt, kmatt, vtm,
                            per_head(qs), per_head(ks), per_head(vs),
                            cache_k[0].reshape(n_pool, page * N_ATT_HEADS, HEAD_V),
                            cache_v[0].reshape(n_pool, page * N_ATT_HEADS, HEAD_V),
                            tq=_tile(seq, 512), tk=_tile(seq, 1024))

    y_prompt = _epilogue(xp, o_att, gatt, ylg, sub, wout, npost, tr=tr)
    y_sample = _epilogue(xs, o_s.reshape(db, D_ATT), gatt_s, ylg_s, sub, wout, npost, tr=db)

    t_all = seq + N_META
    k_prompt = k_rows.reshape(1, 1, t_all, N_ATT_HEADS, HEAD_V)
    v_prompt = v_rows.reshape(1, 1, t_all, N_ATT_HEADS, HEAD_V)
    return (y_prompt[None], y_sample[:, None], k_prompt, v_prompt,
            hlast[0:1][None], convlast[SUBLANES - (CONV_WIDTH - 1):][None, None],
            ks.reshape(1, db, 1, N_ATT_HEADS, HEAD_V), vs.reshape(1, db, 1, N_ATT_HEADS, HEAD_V),
            h_s[None], conv_s.reshape(1, db, CONV_WIDTH - 1, D_LRU))
```

```python
import functools
import math

import jax
import jax.numpy as jnp
from jax import lax
from jax.experimental import pallas as pl
from jax.experimental.pallas import tpu as pltpu

N_META = 16
N_ATT_HEADS = 4
HEAD_DIM = 64
HEAD_V = 2 * HEAD_DIM
D_ATT = N_ATT_HEADS * HEAD_V
D_LRU = 512
N_LRU_BLOCKS = 8
CONV_WIDTH = 4
LRU_C = 8.0
ROPE_THETA = 10000.0
EPS = 1e-6
LAM_INIT = 0.8 - 0.6 * math.exp(-0.3 * 0)
Q_SCALE = HEAD_DIM ** -0.5 * math.log2(math.e)

LANES = 128
SUBLANES = 8
VMEM_LIMIT = 56 * 1024 * 1024
BF16_ROWS = 16

F32 = jnp.float32
BF16 = jnp.bfloat16


def _rms(x, g):
    return x * lax.rsqrt(jnp.mean(x * x, axis=-1, keepdims=True) + EPS) * g


def _silu(x):
    return x * jax.nn.sigmoid(x)


def _first_half_mask(shape):
    lane = lax.broadcasted_iota(jnp.int32, shape, len(shape) - 1)
    return (lane % HEAD_DIM) < (HEAD_DIM // 2)


def _rope_block(zb, cos, sin_signed, first_half):
    half = HEAD_DIM // 2
    partner = jnp.where(first_half, pltpu.roll(zb, LANES - half, axis=1), pltpu.roll(zb, half, axis=1))
    return zb * cos + partner * sin_signed


def _diff_lambda(lq1, lk1, lq2, lk2):
    return (jnp.exp(jnp.sum(lq1[...] * lk1[...])) - jnp.exp(jnp.sum(lq2[...] * lk2[...])) + LAM_INIT)


def _lru_gates(xc, wg_ref, br_ref, bi_ref, lam_ref):
    gates = jnp.dot(xc.astype(BF16), wg_ref[...], preferred_element_type=F32)
    r = jax.nn.sigmoid(gates[:, :D_LRU] + br_ref[...])
    i = jax.nn.sigmoid(gates[:, D_LRU:] + bi_ref[...])
    neg_lam = -lam_ref[...]
    softplus = jnp.maximum(neg_lam, 0.0) + jnp.log1p(jnp.exp(-jnp.abs(neg_lam)))
    log_a = -LRU_C * r * softplus
    a = jnp.exp(log_a)
    one_minus_a2 = 1.0 - jnp.exp(2.0 * log_a)
    root = jnp.where(one_minus_a2 > 0.0, one_minus_a2 * lax.rsqrt(one_minus_a2), 0.0)
    u = root * (i * xc)
    return a, u


def _prompt_pre_kernel(x_ref, meta_ref, npre_ref, win_ref, inv_ref, convw_ref, convb_ref, wg_ref,
                       br_ref, bi_ref, lam_ref,
                       q_ref, katt_ref, vt_ref, gatt_ref, ylg_ref,
                       kmatt_ref, vtm_ref, hlast_ref, convlast_ref, k_hbm, v_hbm,
                       cos_tab, sin_tab, conv_carry, h_carry, kbuf, vbuf, kmbuf, vmbuf, sems, msems, *, tr):
    step = pl.program_id(0)
    nh = N_ATT_HEADS

    def row_copies(slot, tile):
        rows = pl.ds((N_META + tile * tr) * nh, tr * nh)
        return (pltpu.make_async_copy(kbuf.at[slot], k_hbm.at[rows], sems.at[0, slot]),
                pltpu.make_async_copy(vbuf.at[slot], v_hbm.at[rows], sems.at[1, slot]))

    def meta_copies():
        rows = pl.ds(0, N_META * nh)
        return (pltpu.make_async_copy(kmbuf, k_hbm.at[rows], msems.at[0]),
                pltpu.make_async_copy(vmbuf, v_hbm.at[rows], msems.at[1]))

    def stage_rows(kdst, vdst, ks, zv, rows):
        for h in range(nh):
            kdst[pl.ds(h, rows, stride=nh), :] = ks[h][0:rows]
            vdst[pl.ds(h, rows, stride=nh), :] = zv[0:rows, h * HEAD_V:(h + 1) * HEAD_V]

    def rope_tables(pos0, rows):
        base = pos0.astype(F32) * inv_ref[...]
        cb, sb = jnp.cos(base), jnp.sin(base)
        cr, sr = cos_tab[0:rows, :], sin_tab[0:rows, :]
        cos = cr * cb - sr * sb
        sin = sr * cb + cr * sb
        return cos, jnp.where(_first_half_mask((rows, LANES)), -sin, sin)

    def project(x, pos0, rows):
        u = _rms(x, npre_ref[...]).astype(BF16)
        cos, sin_signed = rope_tables(pos0, rows)
        fh = _first_half_mask((rows, LANES))
        zq = jnp.dot(u, win_ref[:, 0:D_ATT], preferred_element_type=F32)
        zk = jnp.dot(u, win_ref[:, D_ATT:2 * D_ATT], preferred_element_type=F32)
        zv = jnp.dot(u, win_ref[:, 2 * D_ATT:3 * D_ATT], preferred_element_type=F32)
        qs, ks = [], []
        for h in range(N_ATT_HEADS):
            sl = slice(h * HEAD_V, (h + 1) * HEAD_V)
            qs.append(_rope_block(zq[:, sl], cos, sin_signed, fh) * Q_SCALE)
            ks.append(_rope_block(zk[:, sl], cos, sin_signed, fh))
        return u, qs, ks, zv

    def lru(xb, glru, rows, store_y):
        xp = jnp.concatenate([conv_carry[...], xb], axis=0)
        off = SUBLANES - (CONV_WIDTH - 1)
        xc = convb_ref[...] + convw_ref[0:1, :] * xp[off:off + rows]
        for j in range(1, CONV_WIDTH):
            xc = xc + convw_ref[j:j + 1, :] * xp[off + j:off + j + rows]
        conv_carry[...] = xp[rows:rows + SUBLANES]
        a, u = _lru_gates(xc, wg_ref, br_ref, bi_ref, lam_ref)
        row = lax.broadcasted_iota(jnp.int32, (rows, D_LRU), 0) % SUBLANES
        d = 1
        while d < SUBLANES:
            valid = row >= d
            u = jnp.where(valid, a * pltpu.roll(u, d, axis=0) + u, u)
            a = jnp.where(valid, a * pltpu.roll(a, d, axis=0), a)
            d *= 2
        h = h_carry[...]
        pending = []
        for g in range(rows // SUBLANES):
            sl = slice(g * SUBLANES, (g + 1) * SUBLANES)
            hg = a[sl] * h + u[sl]
            if store_y:
                pending.append(hg * _silu(glru[sl]))
                if len(pending) == BF16_ROWS // SUBLANES:
                    lo = (g + 1) * SUBLANES - BF16_ROWS
                    ylg_ref[lo:lo + BF16_ROWS, :] = jnp.concatenate(pending, axis=0).astype(BF16)
                    pending = []
            h = jnp.broadcast_to(hg[SUBLANES - 1:SUBLANES, :], (SUBLANES, D_LRU))
        assert not pending
        h_carry[...] = h

    @pl.when(step == 0)
    def _():
        r = lax.broadcasted_iota(jnp.int32, (tr, LANES), 0).astype(F32)
        ang = r * inv_ref[...]
        cos_tab[...] = jnp.cos(ang)
        sin_tab[...] = jnp.sin(ang)
        conv_carry[...] = jnp.zeros_like(conv_carry)
        h_carry[...] = jnp.zeros_like(h_carry)
        xm = jnp.concatenate([meta_ref[...], jnp.zeros((LANES - N_META, meta_ref.shape[1]), F32)], axis=0)
        u, _, ks, zv = project(xm, jnp.zeros((), jnp.int32), LANES)
        for h in range(N_ATT_HEADS):
            sl = slice(h * HEAD_V, (h + 1) * HEAD_V)
            kmatt_ref[h] = ks[h].astype(BF16)
            vtm_ref[h] = zv[:, sl].T.astype(BF16)
        stage_rows(kmbuf, vmbuf, ks, zv, N_META)
        for cp in meta_copies():
            cp.start()
        um = u[0:N_META]
        xb = jnp.dot(um, win_ref[:, 4 * D_ATT:4 * D_ATT + D_LRU], preferred_element_type=F32)
        lru(xb, None, N_META, False)

    u, qs, ks, zv = project(x_ref[...], N_META + step * tr, tr)
    for h in range(N_ATT_HEADS):
        sl = slice(h * HEAD_V, (h + 1) * HEAD_V)
        q_ref[h] = qs[h].astype(BF16)
        katt_ref[h] = ks[h].astype(BF16)
        vt_ref[h] = zv[:, sl].T.astype(BF16)
    slot = step % 2

    @pl.when(step >= 2)
    def _():
        for cp in row_copies(slot, step - 2):
            cp.wait()

    stage_rows(kbuf.at[slot], vbuf.at[slot], ks, zv, tr)
    for cp in row_copies(slot, step):
        cp.start()
    gatt_ref[...] = jnp.dot(u, win_ref[:, 3 * D_ATT:4 * D_ATT], preferred_element_type=F32)
    xb = jnp.dot(u, win_ref[:, 4 * D_ATT:4 * D_ATT + D_LRU], preferred_element_type=F32)
    glru = jnp.dot(u, win_ref[:, 4 * D_ATT + D_LRU:], preferred_element_type=F32)
    lru(xb, glru, tr, True)
    hlast_ref[...] = h_carry[...]
    convlast_ref[...] = conv_carry[...]

    @pl.when(step == pl.num_programs(0) - 1)
    def _():
        for cp in meta_copies():
            cp.wait()
        for cp in row_copies(slot, step):
            cp.wait()

        @pl.when(step >= 1)
        def _():
            for cp in row_copies(1 - slot, step - 1):
                cp.wait()


def _prompt_pre(x, meta, npre, win, inv, convw, convb, wg, br, bi, lam, *, tr):
    s, d = x.shape
    nt = s // tr
    full = lambda a: pl.BlockSpec(a.shape, lambda i: (0,) * a.ndim)
    out_shape = (
        jax.ShapeDtypeStruct((N_ATT_HEADS, s, HEAD_V), BF16),
        jax.ShapeDtypeStruct((N_ATT_HEADS, s, HEAD_V), BF16),
        jax.ShapeDtypeStruct((N_ATT_HEADS, HEAD_V, s), BF16),
        jax.ShapeDtypeStruct((s, D_ATT), F32),
        jax.ShapeDtypeStruct((s, D_LRU), BF16),
        jax.ShapeDtypeStruct((N_ATT_HEADS, LANES, HEAD_V), BF16),
        jax.ShapeDtypeStruct((N_ATT_HEADS, HEAD_V, LANES), BF16),
        jax.ShapeDtypeStruct((SUBLANES, D_LRU), F32),
        jax.ShapeDtypeStruct((SUBLANES, D_LRU), F32),
        jax.ShapeDtypeStruct(((s + N_META) * N_ATT_HEADS, HEAD_V), F32),
        jax.ShapeDtypeStruct(((s + N_META) * N_ATT_HEADS, HEAD_V), F32),
    )
    out_specs = (
        pl.BlockSpec((N_ATT_HEADS, tr, HEAD_V), lambda i: (0, i, 0)),
        pl.BlockSpec((N_ATT_HEADS, tr, HEAD_V), lambda i: (0, i, 0)),
        pl.BlockSpec((N_ATT_HEADS, HEAD_V, tr), lambda i: (0, 0, i)),
        pl.BlockSpec((tr, D_ATT), lambda i: (i, 0)),
        pl.BlockSpec((tr, D_LRU), lambda i: (i, 0)),
    ) + tuple(pl.BlockSpec(o.shape, lambda i, n=len(o.shape): (0,) * n) for o in out_shape[5:9]) + (
        pl.BlockSpec(memory_space=pl.ANY), pl.BlockSpec(memory_space=pl.ANY))
    return pl.pallas_call(
        functools.partial(_prompt_pre_kernel, tr=tr),
        grid=(nt,),
        in_specs=[pl.BlockSpec((tr, d), lambda i: (i, 0))] + [full(a) for a in (meta, npre, win, inv, convw, convb, wg, br, bi, lam)],
        out_specs=out_specs,
        out_shape=out_shape,
        scratch_shapes=[pltpu.VMEM((tr, LANES), F32), pltpu.VMEM((tr, LANES), F32),
                        pltpu.VMEM((SUBLANES, D_LRU), F32), pltpu.VMEM((SUBLANES, D_LRU), F32),
                        pltpu.VMEM((2, tr * N_ATT_HEADS, HEAD_V), F32), pltpu.VMEM((2, tr * N_ATT_HEADS, HEAD_V), F32),
                        pltpu.VMEM((N_META * N_ATT_HEADS, HEAD_V), F32), pltpu.VMEM((N_META * N_ATT_HEADS, HEAD_V), F32),
                        pltpu.SemaphoreType.DMA((2, 2)), pltpu.SemaphoreType.DMA((2,))],
        compiler_params=pltpu.CompilerParams(dimension_semantics=("arbitrary",), vmem_limit_bytes=VMEM_LIMIT),
        name="prompt_pre",
    )(x, meta, npre, win, inv, convw, convb, wg, br, bi, lam)


DECODE_PAGES_PER_REGION = 8
DECODE_RING = 32


def _n_chunks(i, tq, tk):
    return (i * tq) // tk + 1


def _attn_kernel(pt_ref, lq1, lk1, lq2, lk2, q_ref, k_ref, vt_ref, kmeta_ref, vtmeta_ref,
                 qs_ref, kn_ref, vn_ref, ck_hbm, cv_hbm, o_ref, os_ref,
                 m_sc, l_sc, acc_sc, sa_sc, sb_sc, cma_sc, cmb_sc, kring, vring, sems, g_sc, dm_sc, dl_sc, dacc_sc,
                 *, tq, tk, n_pages, total_pages):
    h, i = pl.program_id(0), pl.program_id(1)
    pp, ring, nh = DECODE_PAGES_PER_REGION, DECODE_RING, N_ATT_HEADS
    n_unmasked = _n_chunks(i, tq, tk) - 1
    lam = _diff_lambda(lq1, lk1, lq2, lk2)

    def page_copies(flat, slot):
        page = pt_ref[jnp.minimum(flat, total_pages - 1)]
        return (pltpu.make_async_copy(ck_hbm.at[page], kring.at[slot], sems.at[0, slot]),
                pltpu.make_async_copy(cv_hbm.at[page], vring.at[slot], sems.at[1, slot]))

    @pl.when((h == 0) & (i == 0))
    def _():
        g_sc[0] = 0
        dm_sc[...] = jnp.zeros_like(dm_sc)
        dl_sc[...] = jnp.zeros_like(dl_sc)
        dacc_sc[...] = jnp.zeros_like(dacc_sc)
        for t in range(ring):
            for cp in page_copies(t, t):
                cp.start()

    def decode_wait(g, n):
        for t in range(n):
            for cp in page_copies(g + t, (g + t) % ring):
                cp.wait()

    def decode_start(g, n):
        for t in range(n):
            for cp in page_copies(g + ring + t, (g + t) % ring):
                cp.start()
        g_sc[0] = g + n

    def decode_begin(g):
        gc = jnp.minimum(g, total_pages - pp)
        d = dict(active=g < total_pages,
                 b=gc // n_pages, first=((gc % n_pages) == 0) & (g < total_pages), slot0=g % ring, s=[], pv=None)
        q4 = qs_ref[d["b"]]
        lane = lax.broadcasted_iota(jnp.int32, q4.shape, 1)
        q1, q2 = jnp.where(lane < HEAD_DIM, q4, 0.0), jnp.where(lane >= HEAD_DIM, q4, 0.0)
        d["qrows"] = jnp.concatenate([q1, q2], axis=0).astype(BF16)
        kn = kn_ref[d["b"]]
        d["m0"] = jnp.concatenate([jnp.sum(q1 * kn, axis=1, keepdims=True),
                                   jnp.sum(q2 * kn, axis=1, keepdims=True)], axis=0)
        return d

    def decode_scores(d, t):
        kpage = kring[d["slot0"] + t].astype(BF16)
        d["s"].append(lax.dot_general(d["qrows"], kpage, (((1,), (1,)), ((), ())),
                                      preferred_element_type=F32))

    def decode_softmax(d):
        s, first = jnp.concatenate(d["s"], axis=1), d["first"]
        head_of_col = lax.broadcasted_iota(jnp.int32, s.shape, 1) % nh
        head_of_row = lax.broadcasted_iota(jnp.int32, s.shape, 0) % nh
        s = jnp.where(head_of_col == head_of_row, s, -jnp.inf)
        d["m_old"] = jnp.where(first, d["m0"], dm_sc[:, 0:1])
        d["l_old"] = jnp.where(first, 1.0, dl_sc[:, 0:1])
        d["m_new"] = jnp.maximum(d["m_old"], jnp.max(s, axis=1, keepdims=True))
        d["alpha"] = jnp.exp2(d["m_old"] - d["m_new"])
        p = jnp.exp2(s - d["m_new"])
        d["l_new"] = d["alpha"] * d["l_old"] + jnp.sum(p, axis=1, keepdims=True)
        d["p"] = p.astype(BF16)

    def decode_values(d, t):
        prow = kring.shape[1]
        pv = jnp.dot(d["p"][:, t * prow:(t + 1) * prow], vring[d["slot0"] + t].astype(BF16),
                     preferred_element_type=F32)
        d["pv"] = pv if d["pv"] is None else d["pv"] + pv

    def decode_finish(d):
        active, first, b = d["active"], d["first"], d["b"]
        vn = vn_ref[b]
        acc_old = jnp.where(first, jnp.concatenate([vn, vn], axis=0), dacc_sc[...])
        acc_new = d["alpha"] * acc_old + d["pv"]
        m_keep = jnp.where(active, d["m_new"], d["m_old"])
        l_keep = jnp.where(active, d["l_new"], d["l_old"])
        acc_keep = jnp.where(active, acc_new, acc_old)
        dm_sc[...] = jnp.broadcast_to(m_keep, dm_sc.shape)
        dl_sc[...] = jnp.broadcast_to(l_keep, dl_sc.shape)
        dacc_sc[...] = acc_keep
        o = acc_keep / l_keep
        os_ref[b] = o[0:nh] - lam * o[nh:2 * nh]

    qt = q_ref[0].astype(F32).T
    rows = lax.broadcasted_iota(jnp.int32, qt.shape, 0)
    qb = jnp.concatenate([jnp.where(rows < HEAD_DIM, qt, 0.0), jnp.where(rows >= HEAD_DIM, qt, 0.0)],
                         axis=1).astype(BF16)

    cb = 2 * LANES
    n_cb = 2 * tq // cb
    half = n_cb // 2
    assert n_cb % 2 == 0 and pp % half == 0
    ppb = pp // half

    def score_slab(c, j):
        start = pl.multiple_of(c * tk, tk)
        return jnp.dot(k_ref[0, pl.ds(start, tk), :], qb[:, j * cb:(j + 1) * cb],
                       preferred_element_type=F32)

    def causal_slab(j):
        kpos = n_unmasked * tk + lax.broadcasted_iota(jnp.int32, (tk, cb), 0)
        qpos = i * tq + (j * cb + lax.broadcasted_iota(jnp.int32, (tk, cb), 1)) % tq
        return kpos <= qpos

    def region(cur, c_cur, causal, nxt, c_nxt, g):
        cur_ref, cur_cm = cur
        vtt = vt_ref[0, :, pl.ds(pl.multiple_of(c_cur * tk, tk), tk)]
        d = decode_begin(g)
        for j in range(n_cb):
            sl = slice(j * cb, (j + 1) * cb)
            if j < half:
                for u in range(ppb):
                    decode_scores(d, j * ppb + u)
            if nxt is not None:
                s_nxt = score_slab(c_nxt, j)
                nxt[0][:, sl] = s_nxt
                nxt[1][:, sl] = jnp.max(s_nxt, axis=0, keepdims=True)
            if j == half:
                decode_softmax(d)
            s_cur = cur_ref[:, sl]
            if causal:
                s_cur = jnp.where(causal_slab(j), s_cur, -jnp.inf)
                cmax = jnp.max(s_cur, axis=0, keepdims=True)
            else:
                cmax = cur_cm[:, sl]
            m_old = m_sc[:, sl]
            m_new = jnp.maximum(m_old, cmax)
            alpha = jnp.exp2(m_old - m_new)
            p = jnp.exp2(s_cur - m_new)
            l_sc[:, sl] = alpha * l_sc[:, sl] + jnp.sum(p, axis=0, keepdims=True)
            acc_sc[:, sl] = alpha * acc_sc[:, sl] + jnp.dot(vtt, p.astype(BF16), preferred_element_type=F32)
            m_sc[:, sl] = m_new
            if j >= half:
                for u in range(ppb):
                    decode_values(d, (j - half) * ppb + u)
        decode_finish(d)

    krow = lax.broadcasted_iota(jnp.int32, (LANES, 2 * tq), 0)
    s_meta = jnp.where(krow < N_META, jnp.dot(kmeta_ref[0], qb, preferred_element_type=F32), -jnp.inf)
    m_meta = jnp.max(s_meta, axis=0, keepdims=True)
    p_meta = jnp.exp2(s_meta - m_meta)
    m_sc[...] = m_meta
    l_sc[...] = jnp.sum(p_meta, axis=0, keepdims=True)
    acc_sc[...] = jnp.dot(vtmeta_ref[0], p_meta.astype(BF16), preferred_element_type=F32)

    buf_a, buf_b = (sa_sc, cma_sc), (sb_sc, cmb_sc)
    for j in range(n_cb):
        s0 = score_slab(0, j)
        sa_sc[:, j * cb:(j + 1) * cb] = s0
        cma_sc[:, j * cb:(j + 1) * cb] = jnp.max(s0, axis=0, keepdims=True)

    def pair(t, carry):
        c, g = 2 * t, g_sc[0]
        decode_wait(g, 2 * pp)
        region(buf_a, c, False, buf_b, c + 1, g)
        region(buf_b, c + 1, False, buf_a, c + 2, g + pp)
        decode_start(g, 2 * pp)
        return carry

    lax.fori_loop(0, n_unmasked // 2, pair, 0)

    @pl.when(n_unmasked % 2 == 1)
    def _():
        g = g_sc[0]
        decode_wait(g, 2 * pp)
        region(buf_a, n_unmasked - 1, False, buf_b, n_unmasked, g)
        region(buf_b, n_unmasked, True, None, None, g + pp)
        decode_start(g, 2 * pp)

    @pl.when(n_unmasked % 2 == 0)
    def _():
        g = g_sc[0]
        decode_wait(g, pp)
        region(buf_a, n_unmasked, True, None, None, g)
        decode_start(g, pp)

    o = acc_sc[...] * (1.0 / l_sc[...])
    o_ref[...] = (o[:, :tq] - lam * o[:, tq:]).T

    @pl.when((h == pl.num_programs(0) - 1) & (i == pl.num_programs(1) - 1))
    def _():
        g = g_sc[0]
        for t in range(ring):
            for cp in page_copies(g + t, (g + t) % ring):
                cp.wait()


def _attention(page_table, lq1, lk1, lq2, lk2, q, katt, vt, kmatt, vtm, qs, kn, vn, cache_k, cache_v, *, tq, tk):
    nh, s, _ = q.shape
    db, n_pages = page_table.shape
    _, prow, width = cache_k.shape
    total_pages = db * n_pages
    n_slices = nh * sum(_n_chunks(i, tq, tk) for i in range(s // tq))
    assert n_slices * DECODE_PAGES_PER_REGION >= total_pages, "prompt too short to carry the decode stream"
    assert n_pages % DECODE_PAGES_PER_REGION == 0 and DECODE_RING >= 4 * DECODE_PAGES_PER_REGION
    small = lambda a: pl.BlockSpec(a.shape, lambda h, i, pt: (0,) * a.ndim)
    hbm = pl.BlockSpec(memory_space=pl.ANY)
    grid_spec = pltpu.PrefetchScalarGridSpec(
        num_scalar_prefetch=1,
        grid=(nh, s // tq),
        in_specs=[small(lq1), small(lk1), small(lq2), small(lk2),
                  pl.BlockSpec((1, tq, HEAD_V), lambda h, i, pt: (h, i, 0)),
                  pl.BlockSpec((1, s, HEAD_V), lambda h, i, pt: (h, 0, 0)),
                  pl.BlockSpec((1, HEAD_V, s), lambda h, i, pt: (h, 0, 0)),
                  pl.BlockSpec((1, LANES, HEAD_V), lambda h, i, pt: (h, 0, 0)),
                  pl.BlockSpec((1, HEAD_V, LANES), lambda h, i, pt: (h, 0, 0)),
                  small(qs), small(kn), small(vn), hbm, hbm],
        out_specs=[pl.BlockSpec((tq, HEAD_V), lambda h, i, pt: (i, h)),
                   pl.BlockSpec((db, N_ATT_HEADS, width), lambda h, i, pt: (0, 0, 0))],
        scratch_shapes=[pltpu.VMEM((1, 2 * tq), F32), pltpu.VMEM((1, 2 * tq), F32),
                        pltpu.VMEM((HEAD_V, 2 * tq), F32),
                        pltpu.VMEM((tk, 2 * tq), F32), pltpu.VMEM((tk, 2 * tq), F32),
                        pltpu.VMEM((1, 2 * tq), F32), pltpu.VMEM((1, 2 * tq), F32),
                        pltpu.VMEM((DECODE_RING, prow, width), F32), pltpu.VMEM((DECODE_RING, prow, width), F32),
                        pltpu.SemaphoreType.DMA((2, DECODE_RING)), pltpu.SMEM((1,), jnp.int32),
                        pltpu.VMEM((2 * N_ATT_HEADS, LANES), F32), pltpu.VMEM((2 * N_ATT_HEADS, LANES), F32),
                        pltpu.VMEM((2 * N_ATT_HEADS, width), F32)],
    )
    return pl.pallas_call(
        functools.partial(_attn_kernel, tq=tq, tk=tk, n_pages=n_pages, total_pages=total_pages),
        grid_spec=grid_spec,
        out_shape=(jax.ShapeDtypeStruct((s, D_ATT), F32), jax.ShapeDtypeStruct((db, N_ATT_HEADS, width), F32)),
        compiler_params=pltpu.CompilerParams(dimension_semantics=("arbitrary", "arbitrary"),
                                             vmem_limit_bytes=VMEM_LIMIT),
        name="attention",
    )(page_table.reshape(-1), lq1, lk1, lq2, lk2, q, katt, vt, kmatt, vtm, qs, kn, vn, cache_k, cache_v)


def _epilogue_kernel(x_ref, o_ref, gatt_ref, ylg_ref, sub_ref, wout_ref, npost_ref, y_ref):
    parts = []
    for h in range(N_ATT_HEADS):
        sl = slice(h * HEAD_V, (h + 1) * HEAD_V)
        o = _rms(o_ref[:, sl], sub_ref[...]) * (1.0 - LAM_INIT)
        parts.append((o * _silu(gatt_ref[:, sl])).astype(BF16))
    parts.append(ylg_ref[...].astype(BF16))
    mix = jnp.dot(jnp.concatenate(parts, axis=1), wout_ref[...], preferred_element_type=F32)
    y_ref[...] = x_ref[...] + _rms(mix, npost_ref[...])


def _epilogue(x, o_att, gatt, ylg, sub, wout, npost, *, tr):
    s, d = x.shape
    full = lambda a: pl.BlockSpec(a.shape, lambda i: (0,) * a.ndim)
    rows = lambda w: pl.BlockSpec((tr, w), lambda i: (i, 0))
    return pl.pallas_call(
        _epilogue_kernel,
        grid=(s // tr,),
        in_specs=[rows(d), rows(D_ATT), rows(D_ATT), rows(D_LRU), full(sub), full(wout), full(npost)],
        out_specs=rows(d),
        out_shape=jax.ShapeDtypeStruct((s, d), F32),
        compiler_params=pltpu.CompilerParams(dimension_semantics=("arbitrary",), vmem_limit_bytes=VMEM_LIMIT),
        name="epilogue",
    )(x, o_att, gatt, ylg, sub, wout, npost)


def _sample_pre_kernel(x_ref, npre_ref, win_ref, inv_ref, convw_ref, convb_ref, wg_ref, br_ref, bi_ref,
                       lam_ref, h0_ref, sc_ref,
                       q_ref, k_ref, v_ref, gatt_ref, ylg_ref, h_ref, cnew_ref, *, past_len):
    rows = x_ref.shape[0]
    u = _rms(x_ref[...], npre_ref[...]).astype(BF16)
    z = jnp.dot(u, win_ref[...], preferred_element_type=F32)
    ang = float(past_len) * inv_ref[...]
    cos, sin = jnp.cos(ang), jnp.sin(ang)
    fh = _first_half_mask((rows, LANES))
    sin_signed = jnp.where(fh[0:1], -sin, sin)
    for h in range(N_ATT_HEADS):
        sl = slice(h * HEAD_V, (h + 1) * HEAD_V)
        q_ref[:, sl] = _rope_block(z[:, sl], cos, sin_signed, fh) * Q_SCALE
        k_ref[:, sl] = _rope_block(z[:, D_ATT + h * HEAD_V:D_ATT + (h + 1) * HEAD_V], cos, sin_signed, fh)
    v_ref[...] = z[:, 2 * D_ATT:3 * D_ATT]
    gatt_ref[...] = z[:, 3 * D_ATT:4 * D_ATT]
    xb = z[:, 4 * D_ATT:4 * D_ATT + D_LRU]
    glru = z[:, 4 * D_ATT + D_LRU:]
    xc = convb_ref[...] + convw_ref[CONV_WIDTH - 1:CONV_WIDTH, :] * xb
    for j in range(CONV_WIDTH - 1):
        xc = xc + convw_ref[j:j + 1, :] * sc_ref[:, j * D_LRU:(j + 1) * D_LRU]
    a, uu = _lru_gates(xc, wg_ref, br_ref, bi_ref, lam_ref)
    hn = a * h0_ref[...] + uu
    h_ref[...] = hn
    ylg_ref[...] = hn * _silu(glru)
    cnew_ref[:, 0:(CONV_WIDTH - 2) * D_LRU] = sc_ref[:, D_LRU:]
    cnew_ref[:, (CONV_WIDTH - 2) * D_LRU:] = xb


def _sample_pre(x, npre, win, inv, convw, convb, wg, br, bi, lam, h0, sc, *, past_len):
    rows = x.shape[0]
    out_shape = tuple(jax.ShapeDtypeStruct((rows, w), F32)
                      for w in (D_ATT, D_ATT, D_ATT, D_ATT, D_LRU, D_LRU, (CONV_WIDTH - 1) * D_LRU))
    return pl.pallas_call(
        functools.partial(_sample_pre_kernel, past_len=past_len),
        out_shape=out_shape,
        compiler_params=pltpu.CompilerParams(vmem_limit_bytes=VMEM_LIMIT),
        name="sample_pre",
    )(x, npre, win, inv, convw, convb, wg, br, bi, lam, h0, sc)


def _tile(n, target):
    t = min(n, target)
    assert n % t == 0, (n, t)
    return t


def kernel(x_prompt, x_sample, cache_k, cache_v, state_h, state_conv, page_table, meta_tokens, norm_pre, w_in, conv_w, conv_b, w_gate_r, b_gate_r, w_gate_i, b_gate_i, lru_lambda, lambda_q1, lambda_k1, lambda_q2, lambda_k2, attn_subnorm, w_out, norm_post):
    assert x_prompt.shape[0] == 1 and x_sample.shape[1] == 1 and w_in.shape[0] == 1
    seq, d_model = x_prompt.shape[1], x_prompt.shape[2]
    db = x_sample.shape[0]
    n_pool, page = cache_k.shape[1], cache_k.shape[2]
    past_len = page_table.shape[1] * page

    win = w_in[0].astype(BF16)
    wout = w_out[0].astype(BF16)
    row = lambda a: a.reshape(1, -1)
    npre, npost, sub = row(norm_pre[0]), row(norm_post[0]), row(attn_subnorm[0])
    convw, convb = conv_w[0], row(conv_b[0])
    br, bi, lam = row(b_gate_r[0]), row(b_gate_i[0]), row(lru_lambda[0])
    blocks = jnp.arange(D_LRU) // (D_LRU // N_LRU_BLOCKS)
    same_block = blocks[:, None] == blocks[None, :]
    bdiag = lambda w: jnp.where(same_block, jnp.tile(w.reshape(D_LRU, -1), (1, N_LRU_BLOCKS)), 0.0)
    wg = jnp.concatenate([bdiag(w_gate_r[0]), bdiag(w_gate_i[0])], axis=1).astype(BF16)
    lq1, lk1, lq2, lk2 = row(lambda_q1[0]), row(lambda_k1[0]), row(lambda_q2[0]), row(lambda_k2[0])
    inv = ROPE_THETA ** (-jnp.arange(0, HEAD_DIM, 2, dtype=F32) / HEAD_DIM)
    inv = jnp.tile(inv, LANES // (HEAD_DIM // 2)).reshape(1, LANES)

    xp, xs = x_prompt[0], x_sample[:, 0]
    tr = _tile(seq, 512)
    (q, katt, vt, gatt, ylg, kmatt, vtm, hlast, convlast, k_rows, v_rows) = _prompt_pre(
        xp, meta_tokens, npre, win, inv, convw, convb, wg, br, bi, lam, tr=tr)
    qs, ks, vs, gatt_s, ylg_s, h_s, conv_s = _sample_pre(
        xs, npre, win, inv, convw, convb, wg, br, bi, lam, state_h[0],
        state_conv[0].reshape(db, (CONV_WIDTH - 1) * D_LRU), past_len=past_len)

    per_head = lambda a: a.reshape(db, N_ATT_HEADS, HEAD_V)
    o_att, o_s = _attention(page_table, lq1, lk1, lq2, lk2, q, katt, vt, kmatt, vtm,
                            per_head(qs), per_head(ks), per_head(vs),
                            cache_k[0].reshape(n_pool, page * N_ATT_HEADS, HEAD_V),
                            cache_v[0].reshape(n_pool, page * N_ATT_HEADS, HEAD_V),
                            tq=_tile(seq, 512), tk=_tile(seq, 1024))

    y_prompt = _epilogue(xp, o_att, gatt, ylg, sub, wout, npost, tr=tr)
    y_sample = _epilogue(xs, o_s.reshape(db, D_ATT), gatt_s, ylg_s, sub, wout, npost, tr=db)

    t_all = seq + N_META
    k_prompt = k_rows.reshape(1, 1, t_all, N_ATT_HEADS, HEAD_V)
    v_prompt = v_rows.reshape(1, 1, t_all, N_ATT_HEADS, HEAD_V)
    return (y_prompt[None], y_sample[:, None], k_prompt, v_prompt,
            hlast[0:1][None], convlast[SUBLANES - (CONV_WIDTH - 1):][None, None],
            ks.reshape(1, db, 1, N_ATT_HEADS, HEAD_V), vs.reshape(1, db, 1, N_ATT_HEADS, HEAD_V),
            h_s[None], conv_s.reshape(1, db, CONV_WIDTH - 1, D_LRU))
```
